```python
import math
import jax, jax.numpy as jnp
from jax import lax
import numpy as np

D_MODEL = 4096
BATCH = 4
SEQ = 2048
DEPTH = 4
DEC_BATCH = 8
DEC_SEQ = 1
PAST_LEN = 8192
PAGE_SIZE = 128

HEAD_DIM = 128
D_ATT = D_MODEL // 2
N_ATT_HEADS = D_ATT // HEAD_DIM
D_SSM = D_MODEL // 2
SSM_GROUP = 16
N_SSM_GROUPS = D_SSM // SSM_GROUP
SSM_STATE = 64
N_EXPERT_GROUPS = 4
EXPERTS_PER_GROUP = 8
N_EXPERTS = N_EXPERT_GROUPS * EXPERTS_PER_GROUP
TOP_K_IN_GROUP = 2
D_EXPERT = D_MODEL // 8
PLE_DIM = 256
Q_BLOCK = 128
RMS_EPS = 1e-6
FORGET_BIAS_INIT = 3.0
DT_MIN = 1e-3
DT_MAX = 1e-1
NEG_INF = -1e30
IN_COLS = 3 * D_ATT + N_ATT_HEADS + D_SSM + 2 * D_MODEL
IN_SPLITS = (D_ATT, 2 * D_ATT, 3 * D_ATT, 3 * D_ATT + N_ATT_HEADS,
             3 * D_ATT + N_ATT_HEADS + D_SSM, 3 * D_ATT + N_ATT_HEADS + D_SSM + D_MODEL)

kernel_name = "hybrid_s5_fox_hmoe_step"


def rmsnorm(x, g):
    xf = x.astype(jnp.float32)
    y = xf * lax.rsqrt(jnp.mean(xf * xf, axis=-1, keepdims=True) + RMS_EPS)
    return (y * g.astype(jnp.float32)).astype(x.dtype)


def forgetting_attention(q, k, v, cq, ck, q_pos, k_pos):
    bsz, t, h, hd = q.shape
    qb = min(Q_BLOCK, t)
    nb = -(-t // qb)
    pad = nb * qb - t
    if pad:
        q = jnp.pad(q, ((0, 0), (0, pad), (0, 0), (0, 0)))
        cq = jnp.pad(cq, ((0, 0), (0, pad), (0, 0)))
        q_pos = jnp.pad(q_pos, (0, pad), mode="edge")
    q_blocks = q.reshape(bsz, nb, qb, h, hd).transpose(1, 0, 2, 3, 4)
    cq_blocks = cq.reshape(bsz, nb, qb, h).transpose(1, 0, 3, 2)
    pos_blocks = q_pos.reshape(nb, qb)
    ck_t = ck.transpose(0, 2, 1)[:, :, None, :]
    scale = HEAD_DIM ** -0.5

    def one_block(args):
        qi, cqi, pi = args
        s = jnp.einsum('bqhd,bkhd->bhqk', qi, k).astype(jnp.float32) * scale
        s = s + cqi[..., None] - ck_t
        mask = k_pos[None, :] <= pi[:, None]
        s = jnp.where(mask, s, NEG_INF)
        p = jax.nn.softmax(s, axis=-1)
        return jnp.einsum('bhqk,bkhd->bqhd', p.astype(v.dtype), v)

    out = lax.map(one_block, (q_blocks, cq_blocks, pos_blocks))
    out = out.transpose(1, 0, 2, 3, 4).reshape(bsz, nb * qb, h * hd)
    return out[:, :t]


def _complex_affine_combine(e1, e2):
    a1r, a1i, b1r, b1i = e1
    a2r, a2i, b2r, b2i = e2
    ar = a2r * a1r - a2i * a1i
    ai = a2r * a1i + a2i * a1r
    br = a2r * b1r - a2i * b1i + b2r
    bi = a2r * b1i + a2i * b1r + b2i
    return ar, ai, br, bi


def s5_branch(u, a_re, a_im, log_dt, b_re, b_im, c_re, c_im, d_skip, w_glu, h0_re, h0_im):
    f32 = jnp.float32
    bsz, t, _ = u.shape
    uf = u.astype(f32).reshape(bsz, t, N_SSM_GROUPS, SSM_GROUP)
    a_re = a_re.astype(f32)
    a_im = a_im.astype(f32)
    dt = jnp.exp(log_dt.astype(f32))[:, None]
    lr = a_re * dt
    li = a_im * dt
    mag = jnp.exp(lr)
    ab_re = mag * jnp.cos(li)
    ab_im = mag * jnp.sin(li)
    nr = ab_re - 1.0
    ni = ab_im
    den = a_re * a_re + a_im * a_im
    coef_re = (nr * a_re + ni * a_im) / den
    coef_im = (ni * a_re - nr * a_im) / den
    bu_re = jnp.einsum('btgc,gnc->btgn', uf, b_re.astype(f32))
    bu_im = jnp.einsum('btgc,gnc->btgn', uf, b_im.astype(f32))
    x_re = coef_re * bu_re - coef_im * bu_im
    x_im = coef_re * bu_im + coef_im * bu_re
    shp = x_re.shape
    _, _, s_re, s_im = lax.associative_scan(
        _complex_affine_combine,
        (jnp.broadcast_to(ab_re, shp), jnp.broadcast_to(ab_im, shp), x_re, x_im), axis=1)
    steps = jnp.arange(1, t + 1, dtype=f32)[:, None, None]
    pmag = jnp.exp(lr * steps)
    pang = li * steps
    pw_re = pmag * jnp.cos(pang)
    pw_im = pmag * jnp.sin(pang)
    h0r = h0_re.astype(f32)[:, None]
    h0i = h0_im.astype(f32)[:, None]
    h_re = s_re + pw_re * h0r - pw_im * h0i
    h_im = s_im + pw_re * h0i + pw_im * h0r
    y = (jnp.einsum('btgn,gcn->btgc', h_re, c_re.astype(f32))
         - jnp.einsum('btgn,gcn->btgc', h_im, c_im.astype(f32))
         + d_skip.astype(f32).reshape(N_SSM_GROUPS, SSM_GROUP) * uf)
    y = y.reshape(bsz, t, D_SSM)
    g = jax.nn.gelu(y)
    out = g * jax.nn.sigmoid(g @ w_glu.astype(f32))
    return out.astype(u.dtype), h_re[:, -1], h_im[:, -1]


def hierarchical_moe(h, w_rg, b_rg, w_re, b_re, w_gate, w_up, w_down):
    f32 = jnp.float32
    lead = h.shape[:-1]
    hf = h.reshape(-1, D_MODEL)
    n = hf.shape[0]
    g_prob = jax.nn.softmax((hf @ w_rg).astype(f32) + b_rg.astype(f32), axis=-1)
    g_top, g_idx = lax.top_k(g_prob, 1)
    e_logits = ((hf @ w_re).astype(f32) + b_re.astype(f32)).reshape(n, N_EXPERT_GROUPS, EXPERTS_PER_GROUP)
    e_sel = jnp.take_along_axis(e_logits, g_idx[:, :, None], axis=1)[:, 0]
    e_top, e_idx = lax.top_k(e_sel, TOP_K_IN_GROUP)
    w_k = jax.nn.softmax(e_top, axis=-1) * g_top
    expert_id = g_idx * EXPERTS_PER_GROUP + e_idx
    combine = jnp.sum(jax.nn.one_hot(expert_id, N_EXPERTS, dtype=f32) * w_k[..., None], axis=1)
    a = jnp.einsum('nd,edf->nef', hf, w_gate)
    b = jnp.einsum('nd,edf->nef', hf, w_up)
    hid = jax.nn.silu(a) * b * combine[..., None].astype(hf.dtype)
    out = jnp.einsum('nef,efd->nd', hid, w_down)
    return out.reshape(*lead, D_MODEL)


def trunk_layer(x, p_l, prm, k_past, v_past, lf_past, h0_re, h0_im):
    f32 = jnp.float32
    bsz, t, _ = x.shape
    h = rmsnorm(x, prm['g_mix'])
    proj = h @ prm['w_in']
    q, k, v, f_logit, u, gate_s, gate_a = jnp.split(proj, IN_SPLITS, axis=-1)
    q = rmsnorm(q.reshape(bsz, t, N_ATT_HEADS, HEAD_DIM), prm['g_q'])
    k = rmsnorm(k.reshape(bsz, t, N_ATT_HEADS, HEAD_DIM), prm['g_k'])
    v = v.reshape(bsz, t, N_ATT_HEADS, HEAD_DIM)
    lf = jax.nn.log_sigmoid(f_logit.astype(f32) + prm['b_f'].astype(f32))
    if k_past is None:
        c = jnp.cumsum(lf, axis=1)
        pos = jnp.arange(t, dtype=jnp.int32)
        att = forgetting_attention(q, k, v, c, c, pos, pos)
    else:
        past = k_past.shape[1]
        k_all = jnp.concatenate([k_past.astype(k.dtype), k], axis=1)
        v_all = jnp.concatenate([v_past.astype(v.dtype), v], axis=1)
        c = jnp.cumsum(jnp.concatenate([lf_past.astype(f32), lf], axis=1), axis=1)
        k_pos = jnp.arange(past + t, dtype=jnp.int32)
        q_pos = past + jnp.arange(t, dtype=jnp.int32)
        att = forgetting_attention(q, k_all, v_all, c[:, past:], c, q_pos, k_pos)
    s_out, hT_re, hT_im = s5_branch(u, prm['a_re'], prm['a_im'], prm['log_dt'], prm['b_re'], prm['b_im'],
                                    prm['c_re'], prm['c_im'], prm['d'], prm['w_glu'], h0_re, h0_im)
    merged = (jax.nn.sigmoid(gate_s) * (s_out @ prm['w_br_ssm'])
              + jax.nn.sigmoid(gate_a) * (att @ prm['w_br_att']))
    x = x + merged @ prm['w_out']
    x = x + hierarchical_moe(rmsnorm(x, prm['g_ffn']), prm['rg_w'], prm['rg_b'], prm['re_w'], prm['re_b'],
                             prm['w_gate'], prm['w_up'], prm['w_down'])
    x = x + jax.nn.sigmoid(rmsnorm(x, prm['g_ple']) @ prm['w_ple_gate']) * (p_l @ prm['w_ple_proj'])
    return x, k, v, lf, hT_re, hT_im


def setup_inputs(seed: int = 0) -> dict:
    key = jax.random.key(seed)
    keys = jax.random.split(key, 40)
    f32 = jnp.float32

    def nrm(i, shape, scale=1.0):
        return jax.random.normal(keys[i], shape, f32) * scale

    n_pages = PAST_LEN // PAGE_SIZE
    n_used = DEC_BATCH * n_pages
    n_pool = n_used + max(1, n_used // 4)
    page_table = jax.random.permutation(keys[7], n_pool)[:n_used].reshape(DEC_BATCH, n_pages).astype(jnp.int32)
    a_im_init = jnp.pi * jnp.arange(SSM_STATE, dtype=f32)
    return {
        "x_prompt": nrm(0, (BATCH, SEQ, D_MODEL)),
        "x_sample": nrm(1, (DEC_BATCH, DEC_SEQ, D_MODEL)),
        "cache_k": nrm(2, (DEPTH, n_pool, PAGE_SIZE, N_ATT_HEADS, HEAD_DIM)),
        "cache_v": nrm(3, (DEPTH, n_pool, PAGE_SIZE, N_ATT_HEADS, HEAD_DIM)),
        "cache_lf": jax.nn.log_sigmoid(FORGET_BIAS_INIT + nrm(4, (DEPTH, n_pool, PAGE_SIZE, N_ATT_HEADS))),
        "state_ssm_re": nrm(5, (DEPTH, DEC_BATCH, N_SSM_GROUPS, SSM_STATE), 0.1),
        "state_ssm_im": nrm(6, (DEPTH, DEC_BATCH, N_SSM_GROUPS, SSM_STATE), 0.1),
        "page_table": page_table,
        "p_prompt": nrm(8, (DEPTH, BATCH, SEQ, PLE_DIM)),
        "p_sample": nrm(9, (DEPTH, DEC_BATCH, DEC_SEQ, PLE_DIM)),
        "g_mix": 1.0 + nrm(10, (DEPTH, D_MODEL), 0.02),
        "w_in": nrm(11, (DEPTH, D_MODEL, IN_COLS), D_MODEL ** -0.5),
        "g_q": 1.0 + nrm(12, (DEPTH, HEAD_DIM), 0.02),
        "g_k": 1.0 + nrm(13, (DEPTH, HEAD_DIM), 0.02),
        "b_f": FORGET_BIAS_INIT + nrm(14, (DEPTH, N_ATT_HEADS), 0.1),
        "ssm_a_re": -0.5 + nrm(15, (DEPTH, N_SSM_GROUPS, SSM_STATE), 0.01),
        "ssm_a_im": a_im_init + nrm(16, (DEPTH, N_SSM_GROUPS, SSM_STATE), 0.01),
        "ssm_log_dt": jax.random.uniform(keys[17], (DEPTH, N_SSM_GROUPS), f32,
                                         minval=math.log(DT_MIN), maxval=math.log(DT_MAX)),
        "ssm_b_re": nrm(18, (DEPTH, N_SSM_GROUPS, SSM_STATE, SSM_GROUP), (2 * SSM_GROUP) ** -0.5),
        "ssm_b_im": nrm(19, (DEPTH, N_SSM_GROUPS, SSM_STATE, SSM_GROUP), (2 * SSM_GROUP) ** -0.5),
        "ssm_c_re": nrm(20, (DEPTH, N_SSM_GROUPS, SSM_GROUP, SSM_STATE), (2 * SSM_STATE) ** -0.5),
        "ssm_c_im": nrm(21, (DEPTH, N_SSM_GROUPS, SSM_GROUP, SSM_STATE), (2 * SSM_STATE) ** -0.5),
        "ssm_d": nrm(22, (DEPTH, D_SSM)),
        "w_glu": nrm(23, (DEPTH, D_SSM, D_SSM), D_SSM ** -0.5),
        "w_br_ssm": nrm(24, (DEPTH, D_SSM, D_MODEL), D_SSM ** -0.5),
        "w_br_att": nrm(25, (DEPTH, D_ATT, D_MODEL), D_ATT ** -0.5),
        "w_out": nrm(26, (DEPTH, D_MODEL, D_MODEL), D_MODEL ** -0.5),
        "g_ffn": 1.0 + nrm(27, (DEPTH, D_MODEL), 0.02),
        "router_group_w": nrm(28, (DEPTH, D_MODEL, N_EXPERT_GROUPS), D_MODEL ** -0.5),
        "router_group_b": nrm(29, (DEPTH, N_EXPERT_GROUPS), 0.01),
        "router_expert_w": nrm(30, (DEPTH, D_MODEL, N_EXPERTS), D_MODEL ** -0.5),
        "router_expert_b": nrm(31, (DEPTH, N_EXPERTS), 0.01),
        "w_gate": nrm(32, (DEPTH, N_EXPERTS, D_MODEL, D_EXPERT), D_MODEL ** -0.5),
        "w_up": nrm(33, (DEPTH, N_EXPERTS, D_MODEL, D_EXPERT), D_MODEL ** -0.5),
        "w_down": nrm(34, (DEPTH, N_EXPERTS, D_EXPERT, D_MODEL), D_EXPERT ** -0.5),
        "g_ple": 1.0 + nrm(35, (DEPTH, D_MODEL), 0.02),
        "w_ple_gate": nrm(36, (DEPTH, D_MODEL, D_MODEL), D_MODEL ** -0.5),
        "w_ple_proj": nrm(37, (DEPTH, PLE_DIM, D_MODEL), PLE_DIM ** -0.5),
    }


def reference(x_prompt, x_sample, cache_k, cache_v, cache_lf, state_ssm_re, state_ssm_im, page_table,
              p_prompt, p_sample, g_mix, w_in, g_q, g_k, b_f, ssm_a_re, ssm_a_im, ssm_log_dt,
              ssm_b_re, ssm_b_im, ssm_c_re, ssm_c_im, ssm_d, w_glu, w_br_ssm, w_br_att, w_out, g_ffn,
              router_group_w, router_group_b, router_expert_w, router_expert_b, w_gate, w_up, w_down,
              g_ple, w_ple_gate, w_ple_proj):
    dec_b = x_sample.shape[0]
    n_pages = page_table.shape[1]
    past = n_pages * PAGE_SIZE
    bsz = x_prompt.shape[0]
    zeros_state = jnp.zeros((bsz, N_SSM_GROUPS, SSM_STATE), jnp.float32)
    yp = x_prompt
    ys = x_sample
    kp, vp, lfp, srp, sip = [], [], [], [], []
    kss, vss, lfs, srs, sis = [], [], [], [], []
    for i in range(DEPTH):
        prm = {
            'g_mix': g_mix[i], 'w_in': w_in[i], 'g_q': g_q[i], 'g_k': g_k[i], 'b_f': b_f[i],
            'a_re': ssm_a_re[i], 'a_im': ssm_a_im[i], 'log_dt': ssm_log_dt[i],
            'b_re': ssm_b_re[i], 'b_im': ssm_b_im[i], 'c_re': ssm_c_re[i], 'c_im': ssm_c_im[i],
            'd': ssm_d[i], 'w_glu': w_glu[i], 'w_br_ssm': w_br_ssm[i], 'w_br_att': w_br_att[i],
            'w_out': w_out[i], 'g_ffn': g_ffn[i], 'rg_w': router_group_w[i], 'rg_b': router_group_b[i],
            're_w': router_expert_w[i], 're_b': router_expert_b[i], 'w_gate': w_gate[i], 'w_up': w_up[i],
            'w_down': w_down[i], 'g_ple': g_ple[i], 'w_ple_gate': w_ple_gate[i], 'w_ple_proj': w_ple_proj[i],
        }
        yp, k, v, lf, hr, hi = trunk_layer(yp, p_prompt[i], prm, None, None, None, zeros_state, zeros_state)
        kp.append(k); vp.append(v); lfp.append(lf); srp.append(hr); sip.append(hi)
        k_past = cache_k[i][page_table].reshape(dec_b, past, N_ATT_HEADS, HEAD_DIM)
        v_past = cache_v[i][page_table].reshape(dec_b, past, N_ATT_HEADS, HEAD_DIM)
        lf_past = cache_lf[i][page_table].reshape(dec_b, past, N_ATT_HEADS)
        ys, k, v, lf, hr, hi = trunk_layer(ys, p_sample[i], prm, k_past, v_past, lf_past,
                                           state_ssm_re[i], state_ssm_im[i])
        kss.append(k); vss.append(v); lfs.append(lf); srs.append(hr); sis.append(hi)
    return (yp, ys,
            jnp.stack(kp), jnp.stack(vp), jnp.stack(lfp), jnp.stack(srp), jnp.stack(sip),
            jnp.stack(kss), jnp.stack(vss), jnp.stack(lfs), jnp.stack(srs), jnp.stack(sis))
```

```python
import functools
import math

import jax
import jax.numpy as jnp
from jax import lax
from jax.experimental import pallas as pl
from jax.experimental.pallas import tpu as pltpu

F32 = jnp.float32
BF16 = jnp.bfloat16
I32 = jnp.int32

RMS_EPS = 1e-6
NEG_INF = -1e30
HEAD_DIM = 128
LANES = 128
BF16_SUBLANES = 16
TOP_K_IN_GROUP = 2
V7X_VMEM_LIMIT_BYTES = 58 * 1024 * 1024

ROW_TILE_TARGET = 1024
ELEM_ROWS_MAX = 256
MOE_ROWS = 256
DMA_ROWS = 256
ATT_Q = 512
DEC_PAGES = 4
PREP_PAGES = 8
S5_CHUNK = 256
S5_GROUPS = 16


def _cp(*sem):
    return pltpu.CompilerParams(dimension_semantics=sem, vmem_limit_bytes=V7X_VMEM_LIMIT_BYTES)


def _round_up(x, m):
    return -(-x // m) * m


def _row_divisor(tm, max_rows):
    best = BF16_SUBLANES
    for r in range(BF16_SUBLANES, max_rows + 1, BF16_SUBLANES):
        if tm % r == 0:
            best = r
    return best


def _col_tile(n, target):
    t = min(n, target)
    while n % t:
        t //= 2
    return t


def _split3(x):
    hi = x.astype(BF16)
    r1 = x - hi.astype(F32)
    mid = r1.astype(BF16)
    lo = (r1 - mid.astype(F32)).astype(BF16)
    return hi, mid, lo


def _split2(x):
    hi = x.astype(BF16)
    return hi, (x - hi.astype(F32)).astype(BF16)


def _dot_exact_lhs01(tri, x):
    hi, mid, lo = _split3(x)
    d = functools.partial(jnp.dot, preferred_element_type=F32)
    return d(tri, hi) + d(tri, mid) + d(tri, lo)


def _rmsnorm_kernel(x_ref, g_ref, o_ref):
    x = x_ref[...]
    y = x * lax.rsqrt(jnp.mean(x * x, axis=-1, keepdims=True) + RMS_EPS)
    o_ref[...] = (y * g_ref[...]).astype(o_ref.dtype)


def rmsnorm_rows(x, g, tr, out_dtype):
    m, d = x.shape
    return pl.pallas_call(
        _rmsnorm_kernel,
        grid=(m // tr,),
        in_specs=[pl.BlockSpec((tr, d), lambda i: (i, 0)), pl.BlockSpec((1, d), lambda i: (0, 0))],
        out_specs=pl.BlockSpec((tr, d), lambda i: (i, 0)),
        out_shape=jax.ShapeDtypeStruct((m, d), out_dtype),
        compiler_params=_cp("parallel"),
    )(x, g.reshape(1, d))


def _dot_f32(a, b):
    return jnp.dot(a, b, preferred_element_type=F32, precision=lax.Precision.HIGHEST)


def _fused_matmul_f32_kernel(n_dots, n_extra, epilogue, *refs):
    a_refs = refs[:n_dots]
    w_refs = refs[n_dots:2 * n_dots]
    e_refs = refs[2 * n_dots:2 * n_dots + n_extra]
    o_refs = refs[2 * n_dots + n_extra:]
    accs = [_dot_f32(a_ref[...], w_ref[...]) for a_ref, w_ref in zip(a_refs, w_refs)]
    outs = epilogue(accs, [e_ref[...] for e_ref in e_refs])
    for o_ref, o in zip(o_refs, outs):
        o_ref[...] = o.astype(o_ref.dtype)


def _fused_matmul_kernel(n_dots, n_extra, n_out, epilogue, cast_rows, *refs):
    a_refs = refs[:n_dots]
    w_refs = refs[n_dots:2 * n_dots]
    e_refs = refs[2 * n_dots:2 * n_dots + n_extra]
    o_refs = refs[2 * n_dots + n_extra:2 * n_dots + n_extra + n_out]
    wb_refs = refs[2 * n_dots + n_extra + n_out:]

    @pl.when(pl.program_id(1) == 0)
    def _cast_weights():
        for w_ref, wb_ref in zip(w_refs, wb_refs):
            k = w_ref.shape[0]
            for r0 in range(0, k, cast_rows):
                r1 = min(k, r0 + cast_rows)
                wb_ref[r0:r1, :] = w_ref[r0:r1, :].astype(BF16)

    accs = [jnp.dot(a_ref[...].astype(BF16), wb_ref[...], preferred_element_type=F32)
            for a_ref, wb_ref in zip(a_refs, wb_refs)]
    outs = epilogue(accs, [e_ref[...] for e_ref in e_refs])
    for o_ref, o in zip(o_refs, outs):
        o_ref[...] = o.astype(o_ref.dtype)


def fused_matmul(dots, extras, epilogue, out_dtypes, m, n, tm, tn, f32_dots=False):
    in_specs, args, scratch = [], [], []
    for a, a_lead, _, _, _ in dots:
        k = a.shape[-1]
        if a.ndim == 3:
            in_specs.append(pl.BlockSpec((None, tm, k), lambda j, i, l=a_lead: (l, i, 0)))
        else:
            in_specs.append(pl.BlockSpec((tm, k), lambda j, i: (i, 0)))
        args.append(a)
    for _, _, w, w_lead, w_cb in dots:
        k = w.shape[-2]
        if w.ndim == 3:
            in_specs.append(pl.BlockSpec((None, k, tn), lambda j, i, l=w_lead, c=w_cb: (l, 0, j + c)))
        else:
            in_specs.append(pl.BlockSpec((k, tn), lambda j, i, c=w_cb: (0, j + c)))
        args.append(w)
        if not f32_dots:
            scratch.append(pltpu.VMEM((k, tn), BF16))
    for e, e_cb in extras:
        in_specs.append(pl.BlockSpec((tm, tn), lambda j, i, c=e_cb: (i, j + c)))
        args.append(e)
    n_out = len(out_dtypes)
    if f32_dots:
        kern = functools.partial(_fused_matmul_f32_kernel, len(dots), len(extras), epilogue)
    else:
        kern = functools.partial(_fused_matmul_kernel, len(dots), len(extras), n_out, epilogue, 512)
    outs = pl.pallas_call(
        kern,
        grid=(n // tn, m // tm),
        in_specs=in_specs,
        out_specs=[pl.BlockSpec((tm, tn), lambda j, i: (i, j)) for _ in out_dtypes],
        out_shape=[jax.ShapeDtypeStruct((m, n), dt) for dt in out_dtypes],
        scratch_shapes=scratch,
        compiler_params=_cp("arbitrary", "arbitrary"),
    )(*args)
    return outs


def _qkprep_kernel(n_heads, q_ref, k_ref, v_ref, f_ref, gq_ref, gk_ref, bf_ref,
                   qn_ref, kn_ref, kb_ref, vb_ref, lf_ref):
    gq = gq_ref[...]
    gk = gk_ref[...]
    for h in range(n_heads):
        sl = slice(h * HEAD_DIM, (h + 1) * HEAD_DIM)
        q = q_ref[:, sl]
        qn = q * lax.rsqrt(jnp.mean(q * q, axis=-1, keepdims=True) + RMS_EPS) * gq
        qn_ref[:, sl] = qn.astype(qn_ref.dtype)
        k = k_ref[:, sl]
        kn = k * lax.rsqrt(jnp.mean(k * k, axis=-1, keepdims=True) + RMS_EPS) * gk
        kn_ref[:, sl] = kn
        kb_ref[:, sl] = kn.astype(kb_ref.dtype)
    vb_ref[...] = v_ref[...].astype(vb_ref.dtype)
    z = f_ref[...] + bf_ref[...]
    lf_ref[...] = jnp.minimum(z, 0.0) - jnp.log1p(jnp.exp(-jnp.abs(z)))


def qk_prepare(qkv, f, g_q, g_k, b_f_pad, tr, n_heads, act):
    m = qkv.shape[0]
    da = n_heads * HEAD_DIM
    row = lambda c: pl.BlockSpec((tr, da), lambda i, c=c: (i, c))
    vec = pl.BlockSpec((1, LANES), lambda i: (0, 0))
    big = lambda dt: jax.ShapeDtypeStruct((m, da), dt)
    return pl.pallas_call(
        functools.partial(_qkprep_kernel, n_heads),
        grid=(m // tr,),
        in_specs=[row(0), row(1), row(2), pl.BlockSpec((tr, LANES), lambda i: (i, 0)), vec, vec, vec],
        out_specs=[row(0), row(0), row(0), row(0), pl.BlockSpec((tr, LANES), lambda i: (i, 0))],
        out_shape=[big(act), big(F32), big(act), big(act), jax.ShapeDtypeStruct((m, LANES), F32)],
        compiler_params=_cp("parallel"),
    )(qkv, qkv, qkv, f, g_q.reshape(1, HEAD_DIM), g_k.reshape(1, HEAD_DIM), b_f_pad)


def _cumsum_kernel(x_ref, o_ref, carry_ref):
    @pl.when(pl.program_id(1) == 0)
    def _init():
        carry_ref[...] = jnp.zeros_like(carry_ref)

    lc = x_ref.shape[0]
    r = lax.broadcasted_iota(I32, (lc, lc), 0)
    c = lax.broadcasted_iota(I32, (lc, lc), 1)
    tri = jnp.where(c <= r, 1.0, 0.0).astype(BF16)
    out = _dot_exact_lhs01(tri, x_ref[...]) + carry_ref[...]
    o_ref[...] = out
    carry_ref[...] = out[lc - 1:lc, :]


def cumsum_time(x, n_seq, t, lc):
    m = x.shape[0]
    nb = t // lc
    return pl.pallas_call(
        _cumsum_kernel,
        grid=(n_seq, nb),
        in_specs=[pl.BlockSpec((lc, LANES), lambda b, j: (b * nb + j, 0))],
        out_specs=pl.BlockSpec((lc, LANES), lambda b, j: (b * nb + j, 0)),
        out_shape=jax.ShapeDtypeStruct((n_seq * t, LANES), F32),
        scratch_shapes=[pltpu.VMEM((1, LANES), F32)],
        compiler_params=_cp("parallel", "arbitrary"),
    )(x)


def _flash_kernel(scale, tq, q_ref, k_ref, v_ref, c_ref, ck_ref, o_ref, m_sc, l_sc, acc_sc):
    h = pl.program_id(1)
    qi = pl.program_id(2)
    lane = lax.broadcasted_iota(I32, (tq, LANES), 1)
    cq = jnp.sum(jnp.where(lane == h, c_ref[...], 0.0), axis=-1, keepdims=True)
    m_sc[...] = jnp.full_like(m_sc, NEG_INF)
    l_sc[...] = jnp.zeros_like(l_sc)
    acc_sc[...] = jnp.zeros_like(acc_sc)
    q = q_ref[...]
    row = qi * tq + lax.broadcasted_iota(I32, (tq, tq), 0)
    col0 = lax.broadcasted_iota(I32, (tq, tq), 1)

    def body(ki, carry):
        off = pl.multiple_of(ki * tq, tq)
        k = k_ref[pl.ds(off, tq), :]
        v = v_ref[pl.ds(off, tq), :]
        s = lax.dot_general(q, k, (((1,), (1,)), ((), ())), preferred_element_type=F32) * scale
        s = s + cq - ck_ref[ki]
        s = jnp.where(col0 + ki * tq <= row, s, NEG_INF)
        m_prev = m_sc[...]
        m_new = jnp.maximum(m_prev, jnp.max(s, axis=-1, keepdims=True))
        alpha = jnp.exp(m_prev - m_new)
        p = jnp.exp(s - m_new)
        l_sc[...] = alpha * l_sc[...] + jnp.sum(p, axis=-1, keepdims=True)
        acc_sc[...] = alpha * acc_sc[...] + jnp.dot(p.astype(BF16), v, preferred_element_type=F32)
        m_sc[...] = m_new
        return carry

    lax.fori_loop(0, qi + 1, body, 0)
    o_ref[...] = (acc_sc[...] / l_sc[...]).astype(o_ref.dtype)


def prompt_attention(qn, kb, vb, c, ck_row, n_seq, t, n_heads, m_rows):
    tq = min(ATT_Q, t)
    nq = t // tq
    kv_spec = pl.BlockSpec((t, HEAD_DIM), lambda b, h, i: (b, h))
    return pl.pallas_call(
        functools.partial(_flash_kernel, HEAD_DIM ** -0.5, tq),
        grid=(n_seq, n_heads, nq),
        in_specs=[
            pl.BlockSpec((tq, HEAD_DIM), lambda b, h, i: (b * nq + i, h)),
            kv_spec, kv_spec,
            pl.BlockSpec((tq, LANES), lambda b, h, i: (b * nq + i, 0)),
            pl.BlockSpec((None, None, nq, 1, tq), lambda b, h, i: (b, h, 0, 0, 0)),
        ],
        out_specs=pl.BlockSpec((tq, HEAD_DIM), lambda b, h, i: (b * nq + i, h)),
        out_shape=jax.ShapeDtypeStruct((m_rows, n_heads * HEAD_DIM), BF16),
        scratch_shapes=[pltpu.VMEM((tq, 1), F32), pltpu.VMEM((tq, 1), F32), pltpu.VMEM((tq, HEAD_DIM), F32)],
        compiler_params=_cp("parallel", "parallel", "arbitrary"),
    )(qn, kb, vb, c, ck_row)


def _decay_prefix_kernel(n_pages_step, pt_ref, *refs):
    lf_refs = refs[:n_pages_step]
    o_ref = refs[n_pages_step]
    carry_ref = refs[n_pages_step + 1]

    @pl.when(pl.program_id(1) == 0)
    def _init():
        carry_ref[...] = jnp.zeros_like(carry_ref)

    ps = lf_refs[0].shape[0]
    r = lax.broadcasted_iota(I32, (ps, ps), 0)
    c = lax.broadcasted_iota(I32, (ps, ps), 1)
    upper = jnp.where(c > r, 1.0, 0.0).astype(BF16)
    carry = carry_ref[...]
    for idx in range(n_pages_step):
        x = lf_refs[idx][...]
        suffix = _dot_exact_lhs01(upper, x) + carry
        slot = n_pages_step - 1 - idx
        o_ref[slot * ps:(slot + 1) * ps, :] = suffix
        carry = suffix[0:1, :] + x[0:1, :]
    carry_ref[...] = carry


def decay_prefix(cache_lf, page_table):
    depth, _, ps, nh = cache_lf.shape
    nb, n_pages = page_table.shape
    pp = math.gcd(PREP_PAGES, n_pages)
    steps = n_pages // pp

    def lf_spec(idx):
        def imap(lb, j, pt):
            page = n_pages - 1 - (j * pp + idx)
            return (lb // nb, pt[(lb % nb) * n_pages + page], 0, 0)
        return pl.BlockSpec((None, None, ps, nh), imap)

    grid_spec = pltpu.PrefetchScalarGridSpec(
        num_scalar_prefetch=1,
        grid=(depth * nb, steps),
        in_specs=[lf_spec(idx) for idx in range(pp)],
        out_specs=pl.BlockSpec((None, None, pp * ps, nh),
                               lambda lb, j, pt: (lb // nb, lb % nb, steps - 1 - j, 0)),
        scratch_shapes=[pltpu.VMEM((1, nh), F32)],
    )
    return pl.pallas_call(
        functools.partial(_decay_prefix_kernel, pp),
        grid_spec=grid_spec,
        out_shape=jax.ShapeDtypeStruct((depth, nb, n_pages * ps, nh), F32),
        compiler_params=_cp("parallel", "arbitrary"),
    )(page_table.reshape(-1), *([cache_lf] * pp))


def _decode_kernel(scale, n_pages_step, n_heads, pt_ref, q_ref, kn_ref, vn_ref, lfn_ref, bias_ref, *refs):
    k_refs = refs[:n_pages_step]
    v_refs = refs[n_pages_step:2 * n_pages_step]
    o_ref = refs[2 * n_pages_step]
    m_sc, l_sc, acc_sc = refs[2 * n_pages_step + 1:]
    j = pl.program_id(1)

    @pl.when(j == 0)
    def _init():
        m_sc[...] = jnp.full_like(m_sc, NEG_INF)
        l_sc[...] = jnp.zeros_like(l_sc)
        acc_sc[...] = jnp.zeros_like(acc_sc)

    q = q_ref[...]
    q_hi, q_lo = _split2(q)
    width = k_refs[0].shape[0]
    own = (lax.broadcasted_iota(I32, (n_heads, width), 1) % n_heads
           == lax.broadcasted_iota(I32, (n_heads, width), 0))
    lfn = lfn_ref[...]
    qk = lambda a, b: lax.dot_general(a, b, (((1,), (1,)), ((), ())), preferred_element_type=F32)
    pv = functools.partial(jnp.dot, preferred_element_type=F32)
    for idx in range(n_pages_step):
        k_hi, k_lo = _split2(k_refs[idx][...])
        s = (qk(q_hi, k_hi) + qk(q_lo, k_hi) + qk(q_hi, k_lo)) * scale
        s = jnp.where(own, s + bias_ref[idx] + lfn, NEG_INF)
        m_prev = m_sc[...]
        m_new = jnp.maximum(m_prev, jnp.max(s, axis=-1, keepdims=True))
        alpha = jnp.exp(m_prev - m_new)
        p = jnp.exp(s - m_new)
        l_sc[...] = alpha * l_sc[...] + jnp.sum(p, axis=-1, keepdims=True)
        p_hi, p_lo = _split2(p)
        v_hi, v_lo = _split2(v_refs[idx][...])
        acc_sc[...] = alpha * acc_sc[...] + pv(p_hi, v_hi) + pv(p_lo, v_hi) + pv(p_hi, v_lo)
        m_sc[...] = m_new

    @pl.when(j == pl.num_programs(1) - 1)
    def _finish():
        s = jnp.sum(q * kn_ref[...], axis=-1, keepdims=True) * scale
        m_prev = m_sc[...]
        m_new = jnp.maximum(m_prev, s)
        alpha = jnp.exp(m_prev - m_new)
        p = jnp.exp(s - m_new)
        l = alpha * l_sc[...] + p
        acc = alpha * acc_sc[...] + p * vn_ref[...]
        o_ref[...] = (acc / l).astype(o_ref.dtype)


def decode_attention(q, k_new, v_new, lf_new_tiled, bias, cache_k, cache_v, page_table, layer):
    nb, nh, _ = q.shape
    n_pages = page_table.shape[1]
    width = cache_k.shape[2]
    pp = math.gcd(DEC_PAGES, n_pages)
    steps = n_pages // pp

    def page_spec(idx):
        return pl.BlockSpec((None, None, width, HEAD_DIM),
                            lambda b, j, pt: (layer, pt[b * n_pages + j * pp + idx], 0, 0))

    tok = pl.BlockSpec((None, nh, HEAD_DIM), lambda b, j, pt: (b, 0, 0))
    grid_spec = pltpu.PrefetchScalarGridSpec(
        num_scalar_prefetch=1,
        grid=(nb, steps),
        in_specs=[tok, tok, tok,
                  pl.BlockSpec((None, 1, width), lambda b, j, pt: (b, 0, 0)),
                  pl.BlockSpec((None, pp, 1, width), lambda b, j, pt: (b, j, 0, 0))]
                 + [page_spec(idx) for idx in range(pp)] * 2,
        out_specs=pl.BlockSpec((None, nh, HEAD_DIM), lambda b, j, pt: (b, 0, 0)),
        scratch_shapes=[pltpu.VMEM((nh, 1), F32), pltpu.VMEM((nh, 1), F32), pltpu.VMEM((nh, HEAD_DIM), F32)],
    )
    return pl.pallas_call(
        functools.partial(_decode_kernel, HEAD_DIM ** -0.5, pp, nh),
        grid_spec=grid_spec,
        out_shape=jax.ShapeDtypeStruct((nb, nh, HEAD_DIM), F32),
        compiler_params=_cp("parallel", "arbitrary"),
    )(page_table.reshape(-1), q, k_new, v_new, lf_new_tiled, bias, *([cache_k] * pp), *([cache_v] * pp))


def _s5_dot(a, b_ref_slice):
    if b_ref_slice.dtype == F32:
        return _dot_f32(a, b_ref_slice)
    return jnp.dot(a.astype(BF16), b_ref_slice, preferred_element_type=F32)


def _s5_input(u_ref, b_ref, coef_ref):
    w = coef_ref.shape[-1]
    bu = _s5_dot(u_ref[...], b_ref[...])
    bu_re, bu_im = bu[:, :w], bu[:, w:]
    cr, ci = coef_ref[0:1, :], coef_ref[1:2, :]
    return cr * bu_re - ci * bu_im, cr * bu_im + ci * bu_re


def _s5_output(h_re, h_im, u_ref, c_ref, d_ref):
    w = h_re.shape[-1]
    y = _s5_dot(h_re, c_ref[:w, :]) + _s5_dot(h_im, c_ref[w:, :]) + d_ref[...] * u_ref[...]
    return jax.nn.gelu(y)


def _s5_scan_kernel(u_ref, b_ref, c_ref, coef_ref, ab_ref, d_ref, g_ref, gb_ref, hre_ref, him_ref,
                    xre_sc, xim_sc, carry_sc):
    tc = pl.program_id(2)

    @pl.when(tc == 0)
    def _init():
        carry_sc[...] = jnp.zeros_like(carry_sc)

    x_re, x_im = _s5_input(u_ref, b_ref, coef_ref)
    xre_sc[...] = x_re
    xim_sc[...] = x_im
    ar, ai = ab_ref[0:1, :], ab_ref[1:2, :]

    def step(t, carry):
        hr, hi = carry
        nr = ar * hr - ai * hi + xre_sc[pl.ds(t, 1), :]
        ni = ar * hi + ai * hr + xim_sc[pl.ds(t, 1), :]
        xre_sc[pl.ds(t, 1), :] = nr
        xim_sc[pl.ds(t, 1), :] = ni
        return nr, ni

    hr, hi = lax.fori_loop(0, xre_sc.shape[0], step, (carry_sc[0:1, :], carry_sc[1:2, :]), unroll=8)
    carry_sc[0:1, :] = hr
    carry_sc[1:2, :] = hi
    g = _s5_output(xre_sc[...], xim_sc[...], u_ref, c_ref, d_ref)
    g_ref[...] = g
    gb_ref[...] = g.astype(BF16)

    @pl.when(tc == pl.num_programs(2) - 1)
    def _final_state():
        hre_ref[...] = hr
        him_ref[...] = hi


def s5_scan(rest, bcat, ccat, coef, ab, dskip, n_seq, t, m_rows, d_ssm):
    ngb, cw, w2 = bcat.shape
    w = w2 // 2
    lt = min(S5_CHUNK, t)
    nt = t // lt
    par = lambda shape: pl.BlockSpec((None,) + shape, lambda b, g, j: (g, 0, 0))
    row = pl.BlockSpec((lt, cw), lambda b, g, j: (b * nt + j, g))
    st = pl.BlockSpec((None, 1, w), lambda b, g, j: (b, 0, g))
    return pl.pallas_call(
        _s5_scan_kernel,
        grid=(n_seq, ngb, nt),
        in_specs=[row, par((cw, w2)), par((w2, cw)), par((2, w)), par((2, w)), par((1, cw))],
        out_specs=[row, row, st, st],
        out_shape=[jax.ShapeDtypeStruct((m_rows, d_ssm), F32), jax.ShapeDtypeStruct((m_rows, d_ssm), BF16),
                   jax.ShapeDtypeStruct((n_seq, 1, ngb * w), F32), jax.ShapeDtypeStruct((n_seq, 1, ngb * w), F32)],
        scratch_shapes=[pltpu.VMEM((lt, w), F32), pltpu.VMEM((lt, w), F32), pltpu.VMEM((2, w), F32)],
        compiler_params=_cp("parallel", "parallel", "arbitrary"),
    )(rest, bcat, ccat, coef, ab, dskip)


def _s5_step_kernel(n_real, u_ref, b_ref, c_ref, coef_ref, ab_ref, d_ref, h0re_ref, h0im_ref,
                    g_ref, gb_ref, hre_ref, him_ref):
    x_re, x_im = _s5_input(u_ref, b_ref, coef_ref)
    ar, ai = ab_ref[0:1, :], ab_ref[1:2, :]
    h0r, h0i = h0re_ref[...], h0im_ref[...]
    h_re = x_re + ar * h0r - ai * h0i
    h_im = x_im + ar * h0i + ai * h0r
    g = _s5_output(h_re, h_im, u_ref, c_ref, d_ref)
    g_ref[...] = g
    gb_ref[...] = g.astype(BF16)
    hre_ref[...] = h_re[:n_real, :]
    him_ref[...] = h_im[:n_real, :]


def s5_step(rest, row0, rows, n_real, h0_re, h0_im, bcat, ccat, coef, ab, dskip, d_ssm):
    ngb, cw, w2 = bcat.shape
    w = w2 // 2
    rb = row0 // rows
    par = lambda shape: pl.BlockSpec((None,) + shape, lambda g: (g, 0, 0))
    st = pl.BlockSpec((rows, w), lambda g: (0, g))
    sto = pl.BlockSpec((n_real, w), lambda g: (0, g))
    tile = pl.BlockSpec((rows, cw), lambda g: (0, g))
    return pl.pallas_call(
        functools.partial(_s5_step_kernel, n_real),
        grid=(ngb,),
        in_specs=[pl.BlockSpec((rows, cw), lambda g: (rb, g)), par((cw, w2)), par((w2, cw)), par((2, w)),
                  par((2, w)), par((1, cw)), st, st],
        out_specs=[tile, tile, sto, sto],
        out_shape=[jax.ShapeDtypeStruct((rows, d_ssm), F32), jax.ShapeDtypeStruct((rows, d_ssm), BF16),
                   jax.ShapeDtypeStruct((n_real, ngb * w), F32), jax.ShapeDtypeStruct((n_real, ngb * w), F32)],
        compiler_params=_cp("parallel"),
    )(rest, bcat, ccat, coef, ab, dskip, h0_re, h0_im)


def s5_parameters(a_re, a_im, log_dt, b_re, b_im, c_re, c_im, d_skip):
    g, n, ch = b_re.shape
    gb = math.gcd(S5_GROUPS, g)
    ngb = g // gb
    dt = jnp.exp(log_dt)[:, None]
    lr, li = a_re * dt, a_im * dt
    mag = jnp.exp(lr)
    ab_re, ab_im = mag * jnp.cos(li), mag * jnp.sin(li)
    nr, ni = ab_re - 1.0, ab_im
    den = a_re * a_re + a_im * a_im
    coef_re = (nr * a_re + ni * a_im) / den
    coef_im = (ni * a_re - nr * a_im) / den
    blk = lambda x: x.reshape(ngb, 1, gb * n)
    coef = jnp.concatenate([blk(coef_re), blk(coef_im)], axis=1)
    ab = jnp.concatenate([blk(ab_re), blk(ab_im)], axis=1)
    eye = jnp.eye(gb, dtype=F32)

    def bdiag_in(b):
        bb = b.reshape(ngb, gb, n, ch)
        return jnp.einsum('kgnc,gh->kgchn', bb, eye).reshape(ngb, gb * ch, gb * n)

    def bdiag_out(c):
        cc = c.reshape(ngb, gb, ch, n)
        return jnp.einsum('kgcn,gh->kgnhc', cc, eye).reshape(ngb, gb * n, gb * ch)

    bcat = jnp.concatenate([bdiag_in(b_re), bdiag_in(b_im)], axis=2)
    ccat = jnp.concatenate([bdiag_out(c_re), -bdiag_out(c_im)], axis=1)
    return bcat, ccat, coef, ab, d_skip.reshape(ngb, 1, gb * ch)


def _router_kernel(n_groups, per_group, x_ref, g_ref, w_ref, b_ref, hn_ref, eid_ref, wts_ref):
    x = x_ref[...]
    hn = x * lax.rsqrt(jnp.mean(x * x, axis=-1, keepdims=True) + RMS_EPS) * g_ref[...]
    hn_ref[...] = hn
    logits = jnp.dot(hn, w_ref[...], preferred_element_type=F32, precision=lax.Precision.HIGHEST) + b_ref[...]
    lane = lax.broadcasted_iota(I32, logits.shape, 1).astype(F32)
    far = float(LANES)
    red_max = lambda v: jnp.max(v, axis=-1, keepdims=True)
    red_min = lambda v: jnp.min(v, axis=-1, keepdims=True)
    gmask = lane < n_groups
    gl = jnp.where(gmask, logits, -jnp.inf)
    gmax = red_max(gl)
    g_top = 1.0 / jnp.sum(jnp.where(gmask, jnp.exp(gl - gmax), 0.0), axis=-1, keepdims=True)
    g_idx = red_min(jnp.where(gl == gmax, lane, far))
    lo = n_groups + g_idx * per_group
    el = jnp.where((lane >= lo) & (lane < lo + per_group), logits, -jnp.inf)
    t1 = red_max(el)
    i1 = red_min(jnp.where(el == t1, lane, far))
    el2 = jnp.where(lane == i1, -jnp.inf, el)
    t2 = red_max(el2)
    i2 = red_min(jnp.where(el2 == t2, lane, far))
    e21 = jnp.exp(t2 - t1)
    w1 = 1.0 / (1.0 + e21)
    w2 = e21 * w1
    eid_ref[...] = jnp.where(lane == 0.0, i1 - n_groups, jnp.where(lane == 1.0, i2 - n_groups, 0.0)).astype(I32)
    wts_ref[...] = jnp.where(lane == 0.0, w1 * g_top, jnp.where(lane == 1.0, w2 * g_top, 0.0))


def moe_route(x, g, w_router, b_router, n_groups, per_group, tr):
    m, d = x.shape
    rows = lambda wd: pl.BlockSpec((tr, wd), lambda i: (i, 0))
    return pl.pallas_call(
        functools.partial(_router_kernel, n_groups, per_group),
        grid=(m // tr,),
        in_specs=[rows(d), pl.BlockSpec((1, d), lambda i: (0, 0)), pl.BlockSpec((d, LANES), lambda i: (0, 0)),
                  pl.BlockSpec((1, LANES), lambda i: (0, 0))],
        out_specs=[rows(d), rows(LANES), rows(LANES)],
        out_shape=[jax.ShapeDtypeStruct((m, d), F32), jax.ShapeDtypeStruct((m, LANES), I32),
                   jax.ShapeDtypeStruct((m, LANES), F32)],
        compiler_params=_cp("parallel"),
    )(x, g.reshape(1, d), w_router, b_router.reshape(1, LANES))


def moe_schedule(eid, n_experts, tile_rows):
    m = eid.shape[0]
    e = eid[:, :TOP_K_IN_GROUP].reshape(-1)
    n_assign = e.shape[0]
    n_tiles = -(-n_assign // tile_rows) + n_experts
    onehot = (e[:, None] == jnp.arange(n_experts, dtype=I32)[None, :]).astype(I32)
    counts = jnp.sum(onehot, axis=0)
    rank = jnp.take_along_axis(jnp.cumsum(onehot, axis=0), e[:, None], axis=1)[:, 0] - 1
    padded = -(-counts // tile_rows) * tile_rows
    ends = jnp.cumsum(padded)
    dest = (ends - padded)[e] + rank
    src_token = jnp.zeros((n_tiles * tile_rows,), I32).at[dest].set(jnp.arange(n_assign, dtype=I32) // TOP_K_IN_GROUP)
    tile_row0 = jnp.arange(n_tiles, dtype=I32) * tile_rows
    tile_valid = (tile_row0 < ends[-1]).astype(I32)
    tile_expert = jnp.sum((ends[None, :] <= jnp.minimum(tile_row0, ends[-1] - 1)[:, None]).astype(I32), axis=1)
    tile_first = jnp.concatenate([jnp.ones((1,), I32), (tile_expert[1:] != tile_expert[:-1]).astype(I32)])
    slot_rows = dest.reshape(m, TOP_K_IN_GROUP).T.reshape(-1)
    return src_token, slot_rows, tile_expert, tile_first, tile_valid


def _gather_kernel(rows_step, idx_ref, src_ref, out_ref, sem):
    base = pl.program_id(0) * rows_step

    def copy(r):
        return pltpu.make_async_copy(src_ref.at[pl.ds(idx_ref[base + r], 1)], out_ref.at[pl.ds(base + r, 1)], sem)

    def issue(r, carry):
        copy(r).start()
        return carry

    def drain(r, carry):
        copy(r).wait()
        return carry

    lax.fori_loop(0, rows_step, issue, 0)
    lax.fori_loop(0, rows_step, drain, 0)


def gather_rows(src, idx, rows_step):
    n = idx.shape[0]
    grid_spec = pltpu.PrefetchScalarGridSpec(
        num_scalar_prefetch=1,
        grid=(n // rows_step,),
        in_specs=[pl.BlockSpec(memory_space=pl.ANY)],
        out_specs=pl.BlockSpec(memory_space=pl.ANY),
        scratch_shapes=[pltpu.SemaphoreType.DMA(())],
    )
    return pl.pallas_call(
        functools.partial(_gather_kernel, rows_step),
        grid_spec=grid_spec,
        out_shape=jax.ShapeDtypeStruct((n, src.shape[1]), src.dtype),
        compiler_params=_cp("arbitrary"),
    )(idx, src)


def _moe_up_kernel(te_ref, tf_ref, tv_ref, x_ref, wg_ref, wu_ref, o_ref, wgb_ref, wub_ref):
    t = pl.program_id(1)

    @pl.when(tf_ref[t] == 1)
    def _cast_weights():
        wgb_ref[...] = wg_ref[...].astype(BF16)
        wub_ref[...] = wu_ref[...].astype(BF16)

    @pl.when(tv_ref[t] == 1)
    def _compute():
        x = x_ref[...].astype(BF16)
        a = jnp.dot(x, wgb_ref[...], preferred_element_type=F32)
        b = jnp.dot(x, wub_ref[...], preferred_element_type=F32)
        o_ref[...] = (a * jax.nn.sigmoid(a) * b).astype(o_ref.dtype)

    @pl.when(tv_ref[t] == 0)
    def _unused_tile():
        o_ref[...] = jnp.zeros_like(o_ref)


def _moe_up_f32_kernel(te_ref, tf_ref, tv_ref, x_ref, wg_ref, wu_ref, o_ref):
    t = pl.program_id(1)

    @pl.when(tv_ref[t] == 1)
    def _compute():
        x = x_ref[...]
        a = _dot_f32(x, wg_ref[...])
        o_ref[...] = a * jax.nn.sigmoid(a) * _dot_f32(x, wu_ref[...])

    @pl.when(tv_ref[t] == 0)
    def _unused_tile():
        o_ref[...] = jnp.zeros_like(o_ref)


def moe_up(xs, w_gate, w_up, layer, tile_expert, tile_first, tile_valid, tile_rows, f32_dots=False):
    r, d = xs.shape
    f = w_gate.shape[-1]
    fc = _col_tile(f, 256)
    w_spec = pl.BlockSpec((None, None, d, fc), lambda c, t, te, tf, tv: (layer, te[t], 0, c))
    grid_spec = pltpu.PrefetchScalarGridSpec(
        num_scalar_prefetch=3,
        grid=(f // fc, r // tile_rows),
        in_specs=[pl.BlockSpec((tile_rows, d), lambda c, t, te, tf, tv: (t, 0)), w_spec, w_spec],
        out_specs=pl.BlockSpec((tile_rows, fc), lambda c, t, te, tf, tv: (t, c)),
        scratch_shapes=[] if f32_dots else [pltpu.VMEM((d, fc), BF16), pltpu.VMEM((d, fc), BF16)],
    )
    return pl.pallas_call(
        _moe_up_f32_kernel if f32_dots else _moe_up_kernel, grid_spec=grid_spec,
        out_shape=jax.ShapeDtypeStruct((r, f), F32 if f32_dots else BF16),
        compiler_params=_cp("arbitrary", "arbitrary"),
    )(tile_expert, tile_first, tile_valid, xs, w_gate, w_up)


def _moe_down_kernel(te_ref, tf_ref, tv_ref, h_ref, wd_ref, o_ref, wdb_ref):
    t = pl.program_id(1)

    @pl.when(tf_ref[t] == 1)
    def _cast_weights():
        wdb_ref[...] = wd_ref[...].astype(BF16)

    @pl.when(tv_ref[t] == 1)
    def _compute():
        o_ref[...] = jnp.dot(h_ref[...], wdb_ref[...], preferred_element_type=F32)

    @pl.when(tv_ref[t] == 0)
    def _unused_tile():
        o_ref[...] = jnp.zeros_like(o_ref)


def _moe_down_f32_kernel(te_ref, tf_ref, tv_ref, h_ref, wd_ref, o_ref):
    t = pl.program_id(1)

    @pl.when(tv_ref[t] == 1)
    def _compute():
        o_ref[...] = _dot_f32(h_ref[...], wd_ref[...])

    @pl.when(tv_ref[t] == 0)
    def _unused_tile():
        o_ref[...] = jnp.zeros_like(o_ref)


def moe_down(hid, w_down, layer, tile_expert, tile_first, tile_valid, tile_rows, f32_dots=False):
    r, f = hid.shape
    d = w_down.shape[-1]
    nc = _col_tile(d, 2048)
    grid_spec = pltpu.PrefetchScalarGridSpec(
        num_scalar_prefetch=3,
        grid=(d // nc, r // tile_rows),
        in_specs=[pl.BlockSpec((tile_rows, f), lambda c, t, te, tf, tv: (t, 0)),
                  pl.BlockSpec((None, None, f, nc), lambda c, t, te, tf, tv: (layer, te[t], 0, c))],
        out_specs=pl.BlockSpec((tile_rows, nc), lambda c, t, te, tf, tv: (t, c)),
        scratch_shapes=[] if f32_dots else [pltpu.VMEM((f, nc), BF16)],
    )
    return pl.pallas_call(
        _moe_down_f32_kernel if f32_dots else _moe_down_kernel, grid_spec=grid_spec,
        out_shape=jax.ShapeDtypeStruct((r, d), F32),
        compiler_params=_cp("arbitrary", "arbitrary"),
    )(tile_expert, tile_first, tile_valid, hid, w_down)


def _combine_kernel(x_ref, y0_ref, y1_ref, w_ref, g_ref, x2_ref, hn_ref):
    w = w_ref[...]
    x2 = x_ref[...] + w[:, 0:1] * y0_ref[...] + w[:, 1:2] * y1_ref[...]
    x2_ref[...] = x2
    hn = x2 * lax.rsqrt(jnp.mean(x2 * x2, axis=-1, keepdims=True) + RMS_EPS) * g_ref[...]
    hn_ref[...] = hn.astype(hn_ref.dtype)


def moe_combine(x, y_slots, wts, g, tr, act):
    m, d = x.shape
    nb = m // tr
    rows = pl.BlockSpec((tr, d), lambda i: (i, 0))
    return pl.pallas_call(
        _combine_kernel,
        grid=(nb,),
        in_specs=[rows, rows, pl.BlockSpec((tr, d), lambda i: (i + nb, 0)),
                  pl.BlockSpec((tr, LANES), lambda i: (i, 0)), pl.BlockSpec((1, d), lambda i: (0, 0))],
        out_specs=[rows, rows],
        out_shape=[jax.ShapeDtypeStruct((m, d), F32), jax.ShapeDtypeStruct((m, d), act)],
        compiler_params=_cp("parallel"),
    )(x, y_slots, y_slots, wts, g.reshape(1, d))


def kernel(x_prompt, x_sample, cache_k, cache_v, cache_lf, state_ssm_re, state_ssm_im, page_table, p_prompt, p_sample, g_mix, w_in, g_q, g_k, b_f, ssm_a_re, ssm_a_im, ssm_log_dt, ssm_b_re, ssm_b_im, ssm_c_re, ssm_c_im, ssm_d, w_glu, w_br_ssm, w_br_att, w_out, g_ffn, router_group_w, router_group_b, router_expert_w, router_expert_b, w_gate, w_up, w_down, g_ple, w_ple_gate, w_ple_proj):
    nb_p, t, d = x_prompt.shape
    nb_s = x_sample.shape[0]
    assert x_sample.shape[1] == 1
    depth = w_in.shape[0]
    nh = b_f.shape[1]
    da = nh * HEAD_DIM
    d_ssm = ssm_d.shape[1]
    n_grp, n_state, _ = ssm_b_re.shape[1:]
    n_groups = router_group_w.shape[2]
    n_experts = router_expert_w.shape[2]
    per_group = n_experts // n_groups
    ps = cache_k.shape[2]
    n_pages = page_table.shape[1]
    ple = p_prompt.shape[-1]
    assert n_groups + n_experts <= LANES and nh <= LANES

    bt = nb_p * t
    tm = bt // max(1, round(bt / ROW_TILE_TARGET))
    assert bt % tm == 0 and tm % BF16_SUBLANES == 0 and nb_s % 8 == 0
    tn = _col_tile(da, 512)
    tn_m = _col_tile(d, 256)
    cfg_p = (bt, tm, _row_divisor(tm, ELEM_ROWS_MAX), MOE_ROWS, False)
    cfg_s = (nb_s, nb_s, nb_s, 8, True)

    xp = x_prompt.reshape(bt, d)
    xs = x_sample.reshape(nb_s, d)
    pp_all = p_prompt.reshape(depth, bt, ple)
    ps_all = p_sample.reshape(depth, nb_s, ple)

    cache_k2 = cache_k.reshape(depth, -1, ps * nh, HEAD_DIM)
    cache_v2 = cache_v.reshape(depth, -1, ps * nh, HEAD_DIM)
    past_bias = decay_prefix(cache_lf, page_table).reshape(depth, nb_s, n_pages, 1, ps * nh)
    h0_re = state_ssm_re.reshape(depth, nb_s, -1)
    h0_im = state_ssm_im.reshape(depth, nb_s, -1)
    w_router = jnp.concatenate([router_group_w, router_expert_w,
                                jnp.zeros((depth, d, LANES - n_groups - n_experts), F32)], axis=2)
    b_router = jnp.concatenate([router_group_b, router_expert_b,
                                jnp.zeros((depth, LANES - n_groups - n_experts), F32)], axis=1)
    rest_col0 = 3 * da + nh
    n_rest = w_in.shape[2] - rest_col0
    assert n_rest == d_ssm + 2 * d and da % tn == 0 and d_ssm % tn == 0 and d % tn == 0

    ident = lambda accs, ex: accs
    glu = lambda accs, ex: [ex[0] * jax.nn.sigmoid(accs[0])]
    gated = lambda accs, ex: [jax.nn.sigmoid(ex[0]) * accs[0] + jax.nn.sigmoid(ex[1]) * accs[1]]
    resid = lambda accs, ex: [ex[0] + accs[0]]
    ple_gate = lambda accs, ex: [ex[0] + jax.nn.sigmoid(accs[0]) * accs[1]]

    def project(x, i, w_rest, cfg):
        m, tmc, tr, _, hi = cfg
        act = F32 if hi else BF16
        h = rmsnorm_rows(x, g_mix[i], tr, act)
        (qkv,) = fused_matmul([(h, 0, w_in, i, 0)], [], ident, [F32], m, 3 * da, tmc, tn, hi)
        (f_logit,) = fused_matmul([(h, 0, w_in, i, 3 * da // LANES)], [], ident, [F32], m, LANES, tmc, LANES, hi)
        (rest,) = fused_matmul([(h, 0, w_rest, 0, 0)], [], ident, [F32], m, n_rest, tmc, tn, hi)
        b_f_pad = jnp.concatenate([b_f[i], jnp.zeros((LANES - nh,), F32)]).reshape(1, LANES)
        return (qkv, rest) + tuple(qk_prepare(qkv, f_logit, g_q[i], g_k[i], b_f_pad, tr, nh, act))

    def mix_and_ffn(x, p_all, att, g_f, g_act, rest, i, cfg):
        m, tmc, tr, moe_rows, hi = cfg
        act = F32 if hi else BF16
        (s_out,) = fused_matmul([(g_act, 0, w_glu, i, 0)], [(g_f, 0)], glu, [act], m, d_ssm, tmc, tn, hi)
        (merged,) = fused_matmul([(s_out, 0, w_br_ssm, i, 0), (att, 0, w_br_att, i, 0)],
                                 [(rest, d_ssm // tn_m), (rest, (d_ssm + d) // tn_m)], gated, [act],
                                 m, d, tmc, tn_m, hi)
        (x,) = fused_matmul([(merged, 0, w_out, i, 0)], [(x, 0)], resid, [F32], m, d, tmc, tn, hi)
        hn, eid, wts = moe_route(x, g_ffn[i], w_router[i], b_router[i], n_groups, per_group, tr)
        src_token, slot_rows, tile_e, tile_f, tile_v = moe_schedule(eid, n_experts, moe_rows)
        x_sorted = gather_rows(hn, src_token, math.gcd(DMA_ROWS, src_token.shape[0]))
        hid = moe_up(x_sorted, w_gate, w_up, i, tile_e, tile_f, tile_v, moe_rows, hi)
        y_sorted = moe_down(hid, w_down, i, tile_e, tile_f, tile_v, moe_rows, hi)
        y_slots = gather_rows(y_sorted, slot_rows, math.gcd(DMA_ROWS, slot_rows.shape[0]))
        x, hn2 = moe_combine(x, y_slots, wts, g_ple[i], tr, act)
        (x,) = fused_matmul([(hn2, 0, w_ple_gate, i, 0), (p_all, i, w_ple_proj, i, 0)], [(x, 0)], ple_gate,
                            [F32], m, d, tmc, tn_m, hi)
        return x

    outs = [[] for _ in range(10)]
    for i in range(depth):
        w_rest = w_in[i][:, rest_col0:]
        qkv_p, rest_p, qn, kn_p, kb, vb, lf_p = project(xp, i, w_rest, cfg_p)
        qkv_s, rest_s, qn_s, kn_s, _, vb_s, lf_s = project(xs, i, w_rest, cfg_s)

        c = cumsum_time(lf_p, nb_p, t, min(256, t))
        tq = min(ATT_Q, t)
        ck_row = c[:, :nh].reshape(nb_p, t, nh).transpose(0, 2, 1).reshape(nb_p, nh, t // tq, 1, tq)
        att_p = prompt_attention(qn, kb, vb, c, ck_row, nb_p, t, nh, bt)
        tok3 = lambda a: a.reshape(nb_s, nh, HEAD_DIM)
        lf_new = jnp.tile(lf_s[:, :nh], (1, ps)).reshape(nb_s, 1, ps * nh)
        att_s = decode_attention(tok3(qn_s), tok3(kn_s), tok3(vb_s), lf_new, past_bias[i], cache_k2, cache_v2,
                                 page_table, i).reshape(nb_s, da)

        bcat, ccat, coef, ab, dskip = s5_parameters(ssm_a_re[i], ssm_a_im[i], ssm_log_dt[i], ssm_b_re[i],
                                                    ssm_b_im[i], ssm_c_re[i], ssm_c_im[i], ssm_d[i])
        gp_f, gp_b, hp_re, hp_im = s5_scan(rest_p, bcat.astype(BF16), ccat.astype(BF16), coef, ab, dskip,
                                           nb_p, t, bt, d_ssm)
        gs_f, _, hs_re, hs_im = s5_step(rest_s, 0, nb_s, nb_s, h0_re[i], h0_im[i], bcat, ccat, coef, ab, dskip,
                                        d_ssm)

        xp = mix_and_ffn(xp, pp_all, att_p, gp_f, gp_b, rest_p, i, cfg_p)
        xs = mix_and_ffn(xs, ps_all, att_s, gs_f, gs_f, rest_s, i, cfg_s)

        per_layer = (kn_p.reshape(nb_p, t, nh, HEAD_DIM), qkv_p[:, 2 * da:].reshape(nb_p, t, nh, HEAD_DIM),
                     lf_p[:, :nh].reshape(nb_p, t, nh),
                     hp_re.reshape(nb_p, n_grp, n_state), hp_im.reshape(nb_p, n_grp, n_state),
                     kn_s.reshape(nb_s, 1, nh, HEAD_DIM), qkv_s[:, 2 * da:].reshape(nb_s, 1, nh, HEAD_DIM),
                     lf_s[:, :nh].reshape(nb_s, 1, nh),
                     hs_re.reshape(nb_s, n_grp, n_state), hs_im.reshape(nb_s, n_grp, n_state))
        for lst, val in zip(outs, per_layer):
            lst.append(val)

    return (xp.reshape(nb_p, t, d), xs.reshape(nb_s, 1, d)) + tuple(jnp.stack(lst) for lst in outs)
```

```python
import functools
import math

import jax
import jax.numpy as jnp
from jax import lax
from jax.experimental import pallas as pl
from jax.experimental.pallas import tpu as pltpu

F32 = jnp.float32
BF16 = jnp.bfloat16
I32 = jnp.int32

RMS_EPS = 1e-6
NEG_INF = -1e30
HEAD_DIM = 128
LANES = 128
BF16_SUBLANES = 16
TOP_K_IN_GROUP = 2
V7X_VMEM_LIMIT_BYTES = 58 * 1024 * 1024

ROW_TILE_TARGET = 1024
ELEM_ROWS_MAX = 256
MOE_ROWS = 256
DMA_ROWS = 256
ATT_Q = 512
DEC_PAGES = 4
PREP_PAGES = 8
S5_CHUNK = 256
S5_GROUPS = 16


def _cp(*sem):
    return pltpu.CompilerParams(dimension_semantics=sem, vmem_limit_bytes=V7X_VMEM_LIMIT_BYTES)


def _round_up(x, m):
    return -(-x // m) * m


def _row_divisor(tm, max_rows):
    best = BF16_SUBLANES
    for r in range(BF16_SUBLANES, max_rows + 1, BF16_SUBLANES):
        if tm % r == 0:
            best = r
    return best


def _col_tile(n, target):
    t = min(n, target)
    while n % t:
        t //= 2
    return t


def _split3(x):
    hi = x.astype(BF16)
    r1 = x - hi.astype(F32)
    mid = r1.astype(BF16)
    lo = (r1 - mid.astype(F32)).astype(BF16)
    return hi, mid, lo


def _split2(x):
    hi = x.astype(BF16)
    return hi, (x - hi.astype(F32)).astype(BF16)


def _dot_exact_lhs01(tri, x):
    hi, mid, lo = _split3(x)
    d = functools.partial(jnp.dot, preferred_element_type=F32)
    return d(tri, hi) + d(tri, mid) + d(tri, lo)


def _rmsnorm_kernel(x_ref, g_ref, o_ref):
    x = x_ref[...]
    y = x * lax.rsqrt(jnp.mean(x * x, axis=-1, keepdims=True) + RMS_EPS)
    o_ref[...] = (y * g_ref[...]).astype(o_ref.dtype)


def rmsnorm_rows(x, g, tr, out_dtype):
    m, d = x.shape
    return pl.pallas_call(
        _rmsnorm_kernel,
        grid=(m // tr,),
        in_specs=[pl.BlockSpec((tr, d), lambda i: (i, 0)), pl.BlockSpec((1, d), lambda i: (0, 0))],
        out_specs=pl.BlockSpec((tr, d), lambda i: (i, 0)),
        out_shape=jax.ShapeDtypeStruct((m, d), out_dtype),
        compiler_params=_cp("parallel"),
    )(x, g.reshape(1, d))


def _dot_f32(a, b):
    return jnp.dot(a, b, preferred_element_type=F32, precision=lax.Precision.HIGHEST)


def _fused_matmul_f32_kernel(n_dots, n_extra, epilogue, *refs):
    a_refs = refs[:n_dots]
    w_refs = refs[n_dots:2 * n_dots]
    e_refs = refs[2 * n_dots:2 * n_dots + n_extra]
    o_refs = refs[2 * n_dots + n_extra:]
    accs = [_dot_f32(a_ref[...], w_ref[...]) for a_ref, w_ref in zip(a_refs, w_refs)]
    outs = epilogue(accs, [e_ref[...] for e_ref in e_refs])
    for o_ref, o in zip(o_refs, outs):
        o_ref[...] = o.astype(o_ref.dtype)


def _fused_matmul_kernel(n_dots, n_extra, n_out, epilogue, cast_rows, *refs):
    a_refs = refs[:n_dots]
    w_refs = refs[n_dots:2 * n_dots]
    e_refs = refs[2 * n_dots:2 * n_dots + n_extra]
    o_refs = refs[2 * n_dots + n_extra:2 * n_dots + n_extra + n_out]
    wb_refs = refs[2 * n_dots + n_extra + n_out:]

    @pl.when(pl.program_id(1) == 0)
    def _cast_weights():
        for w_ref, wb_ref in zip(w_refs, wb_refs):
            k = w_ref.shape[0]
            for r0 in range(0, k, cast_rows):
                r1 = min(k, r0 + cast_rows)
                wb_ref[r0:r1, :] = w_ref[r0:r1, :].astype(BF16)

    accs = [jnp.dot(a_ref[...].astype(BF16), wb_ref[...], preferred_element_type=F32)
            for a_ref, wb_ref in zip(a_refs, wb_refs)]
    outs = epilogue(accs, [e_ref[...] for e_ref in e_refs])
    for o_ref, o in zip(o_refs, outs):
        o_ref[...] = o.astype(o_ref.dtype)


def fused_matmul(dots, extras, epilogue, out_dtypes, m, n, tm, tn, f32_dots=False):
    in_specs, args, scratch = [], [], []
    for a, a_lead, _, _, _ in dots:
        k = a.shape[-1]
        if a.ndim == 3:
            in_specs.append(pl.BlockSpec((None, tm, k), lambda j, i, l=a_lead: (l, i, 0)))
        else:
            in_specs.append(pl.BlockSpec((tm, k), lambda j, i: (i, 0)))
        args.append(a)
    for _, _, w, w_lead, w_cb in dots:
        k = w.shape[-2]
        if w.ndim == 3:
            in_specs.append(pl.BlockSpec((None, k, tn), lambda j, i, l=w_lead, c=w_cb: (l, 0, j + c)))
        else:
            in_specs.append(pl.BlockSpec((k, tn), lambda j, i, c=w_cb: (0, j + c)))
        args.append(w)
        if not f32_dots:
            scratch.append(pltpu.VMEM((k, tn), BF16))
    for e, e_cb in extras:
        in_specs.append(pl.BlockSpec((tm, tn), lambda j, i, c=e_cb: (i, j + c)))
        args.append(e)
    n_out = len(out_dtypes)
    if f32_dots:
        kern = functools.partial(_fused_matmul_f32_kernel, len(dots), len(extras), epilogue)
    else:
        kern = functools.partial(_fused_matmul_kernel, len(dots), len(extras), n_out, epilogue, 512)
    outs = pl.pallas_call(
        kern,
        grid=(n // tn, m // tm),
        in_specs=in_specs,
        out_specs=[pl.BlockSpec((tm, tn), lambda j, i: (i, j)) for _ in out_dtypes],
        out_shape=[jax.ShapeDtypeStruct((m, n), dt) for dt in out_dtypes],
        scratch_shapes=scratch,
        compiler_params=_cp("arbitrary", "arbitrary"),
    )(*args)
    return outs


def _qkprep_kernel(n_heads, q_ref, k_ref, v_ref, f_ref, gq_ref, gk_ref, bf_ref,
                   qn_ref, kn_ref, kb_ref, vb_ref, lf_ref):
    gq = gq_ref[...]
    gk = gk_ref[...]
    for h in range(n_heads):
        sl = slice(h * HEAD_DIM, (h + 1) * HEAD_DIM)
        q = q_ref[:, sl]
        qn = q * lax.rsqrt(jnp.mean(q * q, axis=-1, keepdims=True) + RMS_EPS) * gq
        qn_ref[:, sl] = qn.astype(qn_ref.dtype)
        k = k_ref[:, sl]
        kn = k * lax.rsqrt(jnp.mean(k * k, axis=-1, keepdims=True) + RMS_EPS) * gk
        kn_ref[:, sl] = kn
        kb_ref[:, sl] = kn.astype(kb_ref.dtype)
    vb_ref[...] = v_ref[...].astype(vb_ref.dtype)
    z = f_ref[...] + bf_ref[...]
    lf_ref[...] = jnp.minimum(z, 0.0) - jnp.log1p(jnp.exp(-jnp.abs(z)))


def qk_prepare(qkv, f, g_q, g_k, b_f_pad, tr, n_heads, act):
    m = qkv.shape[0]
    da = n_heads * HEAD_DIM
    row = lambda c: pl.BlockSpec((tr, da), lambda i, c=c: (i, c))
    vec = pl.BlockSpec((1, LANES), lambda i: (0, 0))
    big = lambda dt: jax.ShapeDtypeStruct((m, da), dt)
    return pl.pallas_call(
        functools.partial(_qkprep_kernel, n_heads),
        grid=(m // tr,),
        in_specs=[row(0), row(1), row(2), pl.BlockSpec((tr, LANES), lambda i: (i, 0)), vec, vec, vec],
        out_specs=[row(0), row(0), row(0), row(0), pl.BlockSpec((tr, LANES), lambda i: (i, 0))],
        out_shape=[big(act), big(F32), big(act), big(act), jax.ShapeDtypeStruct((m, LANES), F32)],
        compiler_params=_cp("parallel"),
    )(qkv, qkv, qkv, f, g_q.reshape(1, HEAD_DIM), g_k.reshape(1, HEAD_DIM), b_f_pad)


def _cumsum_kernel(x_ref, o_ref, carry_ref):
    @pl.when(pl.program_id(1) == 0)
    def _init():
        carry_ref[...] = jnp.zeros_like(carry_ref)

    lc = x_ref.shape[0]
    r = lax.broadcasted_iota(I32, (lc, lc), 0)
    c = lax.broadcasted_iota(I32, (lc, lc), 1)
    tri = jnp.where(c <= r, 1.0, 0.0).astype(BF16)
    out = _dot_exact_lhs01(tri, x_ref[...]) + carry_ref[...]
    o_ref[...] = out
    carry_ref[...] = out[lc - 1:lc, :]


def cumsum_time(x, n_seq, t, lc):
    m = x.shape[0]
    nb = t // lc
    return pl.pallas_call(
        _cumsum_kernel,
        grid=(n_seq, nb),
        in_specs=[pl.BlockSpec((lc, LANES), lambda b, j: (b * nb + j, 0))],
        out_specs=pl.BlockSpec((lc, LANES), lambda b, j: (b * nb + j, 0)),
        out_shape=jax.ShapeDtypeStruct((n_seq * t, LANES), F32),
        scratch_shapes=[pltpu.VMEM((1, LANES), F32)],
        compiler_params=_cp("parallel", "arbitrary"),
    )(x)


def _flash_kernel(scale, tq, q_ref, k_ref, v_ref, c_ref, ck_ref, o_ref, m_sc, l_sc, acc_sc):
    h = pl.program_id(1)
    qi = pl.program_id(2)
    lane = lax.broadcasted_iota(I32, (tq, LANES), 1)
    cq = jnp.sum(jnp.where(lane == h, c_ref[...], 0.0), axis=-1, keepdims=True)
    m_sc[...] = jnp.full_like(m_sc, NEG_INF)
    l_sc[...] = jnp.zeros_like(l_sc)
    acc_sc[...] = jnp.zeros_like(acc_sc)
    q = q_ref[...]

    def block(ki, diagonal):
        off = pl.multiple_of(ki * tq, tq)
        k = k_ref[pl.ds(off, tq), :]
        v = v_ref[pl.ds(off, tq), :]
        s = lax.dot_general(q, k, (((1,), (1,)), ((), ())), preferred_element_type=F32) * scale
        s = s + cq - ck_ref[ki]
        if diagonal:
            causal = (lax.broadcasted_iota(I32, (tq, tq), 1) <= lax.broadcasted_iota(I32, (tq, tq), 0))
            s = jnp.where(causal, s, NEG_INF)
        m_prev = m_sc[...]
        m_new = jnp.maximum(m_prev, jnp.max(s, axis=-1, keepdims=True))
        alpha = jnp.exp(m_prev - m_new)
        p = jnp.exp(s - m_new)
        l_sc[...] = alpha * l_sc[...] + jnp.sum(p, axis=-1, keepdims=True)
        acc_sc[...] = alpha * acc_sc[...] + jnp.dot(p.astype(BF16), v, preferred_element_type=F32)
        m_sc[...] = m_new

    def body(ki, carry):
        block(ki, False)
        return carry

    lax.fori_loop(0, qi, body, 0)
    block(qi, True)
    o_ref[...] = (acc_sc[...] / l_sc[...]).astype(o_ref.dtype)


def prompt_attention(qn, kb, vb, c, ck_row, n_seq, t, n_heads, m_rows):
    tq = min(ATT_Q, t)
    nq = t // tq
    kv_spec = pl.BlockSpec((t, HEAD_DIM), lambda b, h, i: (b, h))
    return pl.pallas_call(
        functools.partial(_flash_kernel, HEAD_DIM ** -0.5, tq),
        grid=(n_seq, n_heads, nq),
        in_specs=[
            pl.BlockSpec((tq, HEAD_DIM), lambda b, h, i: (b * nq + i, h)),
            kv_spec, kv_spec,
            pl.BlockSpec((tq, LANES), lambda b, h, i: (b * nq + i, 0)),
            pl.BlockSpec((None, None, nq, 1, tq), lambda b, h, i: (b, h, 0, 0, 0)),
        ],
        out_specs=pl.BlockSpec((tq, HEAD_DIM), lambda b, h, i: (b * nq + i, h)),
        out_shape=jax.ShapeDtypeStruct((m_rows, n_heads * HEAD_DIM), BF16),
        scratch_shapes=[pltpu.VMEM((tq, 1), F32), pltpu.VMEM((tq, 1), F32), pltpu.VMEM((tq, HEAD_DIM), F32)],
        compiler_params=_cp("parallel", "parallel", "arbitrary"),
    )(qn, kb, vb, c, ck_row)


def _decay_prefix_kernel(n_pages_step, pt_ref, *refs):
    lf_refs = refs[:n_pages_step]
    o_ref = refs[n_pages_step]
    carry_ref = refs[n_pages_step + 1]

    @pl.when(pl.program_id(1) == 0)
    def _init():
        carry_ref[...] = jnp.zeros_like(carry_ref)

    ps = lf_refs[0].shape[0]
    r = lax.broadcasted_iota(I32, (ps, ps), 0)
    c = lax.broadcasted_iota(I32, (ps, ps), 1)
    upper = jnp.where(c > r, 1.0, 0.0).astype(BF16)
    carry = carry_ref[...]
    for idx in range(n_pages_step):
        x = lf_refs[idx][...]
        suffix = _dot_exact_lhs01(upper, x) + carry
        slot = n_pages_step - 1 - idx
        o_ref[slot * ps:(slot + 1) * ps, :] = suffix
        carry = suffix[0:1, :] + x[0:1, :]
    carry_ref[...] = carry


def decay_prefix(cache_lf, page_table):
    depth, _, ps, nh = cache_lf.shape
    nb, n_pages = page_table.shape
    pp = math.gcd(PREP_PAGES, n_pages)
    steps = n_pages // pp

    def lf_spec(idx):
        def imap(lb, j, pt):
            page = n_pages - 1 - (j * pp + idx)
            return (lb // nb, pt[(lb % nb) * n_pages + page], 0, 0)
        return pl.BlockSpec((None, None, ps, nh), imap)

    grid_spec = pltpu.PrefetchScalarGridSpec(
        num_scalar_prefetch=1,
        grid=(depth * nb, steps),
        in_specs=[lf_spec(idx) for idx in range(pp)],
        out_specs=pl.BlockSpec((None, None, pp * ps, nh),
                               lambda lb, j, pt: (lb // nb, lb % nb, steps - 1 - j, 0)),
        scratch_shapes=[pltpu.VMEM((1, nh), F32)],
    )
    return pl.pallas_call(
        functools.partial(_decay_prefix_kernel, pp),
        grid_spec=grid_spec,
        out_shape=jax.ShapeDtypeStruct((depth, nb, n_pages * ps, nh), F32),
        compiler_params=_cp("parallel", "arbitrary"),
    )(page_table.reshape(-1), *([cache_lf] * pp))


def _decode_kernel(scale, n_pages_step, n_heads, pt_ref, q_ref, kn_ref, vn_ref, lfn_ref, bias_ref, *refs):
    k_refs = refs[:n_pages_step]
    v_refs = refs[n_pages_step:2 * n_pages_step]
    o_ref = refs[2 * n_pages_step]
    m_sc, l_sc, acc_sc = refs[2 * n_pages_step + 1:]
    j = pl.program_id(1)

    @pl.when(j == 0)
    def _init():
        m_sc[...] = jnp.full_like(m_sc, NEG_INF)
        l_sc[...] = jnp.zeros_like(l_sc)
        acc_sc[...] = jnp.zeros_like(acc_sc)

    q = q_ref[...]
    q_hi, q_lo = _split2(q)
    width = k_refs[0].shape[0]
    own = (lax.broadcasted_iota(I32, (n_heads, width), 1) % n_heads
           == lax.broadcasted_iota(I32, (n_heads, width), 0))
    lfn = lfn_ref[...]
    qk = lambda a, b: lax.dot_general(a, b, (((1,), (1,)), ((), ())), preferred_element_type=F32)
    pv = functools.partial(jnp.dot, preferred_element_type=F32)
    for idx in range(n_pages_step):
        k_hi, k_lo = _split2(k_refs[idx][...])
        s = (qk(q_hi, k_hi) + qk(q_lo, k_hi) + qk(q_hi, k_lo)) * scale
        s = jnp.where(own, s + bias_ref[idx] + lfn, NEG_INF)
        m_prev = m_sc[...]
        m_new = jnp.maximum(m_prev, jnp.max(s, axis=-1, keepdims=True))
        alpha = jnp.exp(m_prev - m_new)
        p = jnp.exp(s - m_new)
        l_sc[...] = alpha * l_sc[...] + jnp.sum(p, axis=-1, keepdims=True)
        p_hi, p_lo = _split2(p)
        v_hi, v_lo = _split2(v_refs[idx][...])
        acc_sc[...] = alpha * acc_sc[...] + pv(p_hi, v_hi) + pv(p_lo, v_hi) + pv(p_hi, v_lo)
        m_sc[...] = m_new

    @pl.when(j == pl.num_programs(1) - 1)
    def _finish():
        s = jnp.sum(q * kn_ref[...], axis=-1, keepdims=True) * scale
        m_prev = m_sc[...]
        m_new = jnp.maximum(m_prev, s)
        alpha = jnp.exp(m_prev - m_new)
        p = jnp.exp(s - m_new)
        l = alpha * l_sc[...] + p
        acc = alpha * acc_sc[...] + p * vn_ref[...]
        o_ref[...] = (acc / l).astype(o_ref.dtype)


def decode_attention(q, k_new, v_new, lf_new_tiled, bias, cache_k, cache_v, page_table, layer):
    nb, nh, _ = q.shape
    n_pages = page_table.shape[1]
    width = cache_k.shape[2]
    pp = math.gcd(DEC_PAGES, n_pages)
    steps = n_pages // pp

    def page_spec(idx):
        return pl.BlockSpec((None, None, width, HEAD_DIM),
                            lambda b, j, pt: (layer, pt[b * n_pages + j * pp + idx], 0, 0))

    tok = pl.BlockSpec((None, nh, HEAD_DIM), lambda b, j, pt: (b, 0, 0))
    grid_spec = pltpu.PrefetchScalarGridSpec(
        num_scalar_prefetch=1,
        grid=(nb, steps),
        in_specs=[tok, tok, tok,
                  pl.BlockSpec((None, 1, width), lambda b, j, pt: (b, 0, 0)),
                  pl.BlockSpec((None, pp, 1, width), lambda b, j, pt: (b, j, 0, 0))]
                 + [page_spec(idx) for idx in range(pp)] * 2,
        out_specs=pl.BlockSpec((None, nh, HEAD_DIM), lambda b, j, pt: (b, 0, 0)),
        scratch_shapes=[pltpu.VMEM((nh, 1), F32), pltpu.VMEM((nh, 1), F32), pltpu.VMEM((nh, HEAD_DIM), F32)],
    )
    return pl.pallas_call(
        functools.partial(_decode_kernel, HEAD_DIM ** -0.5, pp, nh),
        grid_spec=grid_spec,
        out_shape=jax.ShapeDtypeStruct((nb, nh, HEAD_DIM), F32),
        compiler_params=_cp("parallel", "arbitrary"),
    )(page_table.reshape(-1), q, k_new, v_new, lf_new_tiled, bias, *([cache_k] * pp), *([cache_v] * pp))


def _s5_dot(a, b_ref_slice):
    if b_ref_slice.dtype == F32:
        return _dot_f32(a, b_ref_slice)
    return jnp.dot(a.astype(BF16), b_ref_slice, preferred_element_type=F32)


def _s5_input(u_ref, b_ref, coef_ref):
    w = coef_ref.shape[-1]
    bu = _s5_dot(u_ref[...], b_ref[...])
    bu_re, bu_im = bu[:, :w], bu[:, w:]
    cr, ci = coef_ref[0:1, :], coef_ref[1:2, :]
    return cr * bu_re - ci * bu_im, cr * bu_im + ci * bu_re


def _s5_output(h_re, h_im, u_ref, c_ref, d_ref):
    w = h_re.shape[-1]
    y = _s5_dot(h_re, c_ref[:w, :]) + _s5_dot(h_im, c_ref[w:, :]) + d_ref[...] * u_ref[...]
    return jax.nn.gelu(y)


def _s5_scan_kernel(u_ref, b_ref, c_ref, coef_ref, ab_ref, d_ref, g_ref, gb_ref, hre_ref, him_ref,
                    xre_sc, xim_sc, carry_sc, cin_sc):
    tc = pl.program_id(2)

    @pl.when(tc == 0)
    def _init():
        carry_sc[...] = jnp.zeros_like(carry_sc)

    x_re, x_im = _s5_input(u_ref, b_ref, coef_ref)
    ar, ai = ab_ref[0:1, :], ab_ref[1:2, :]
    nk, lt, _ = xre_sc.shape
    w = nk * LANES
    nseg = 8
    seg = lt // nseg

    def load_rows(sc, i):
        return jnp.concatenate([sc[k, pl.ds(i, nseg, stride=seg), :] for k in range(nk)], axis=1)

    def store_rows(sc, i, val):
        for k in range(nk):
            sc[k, pl.ds(i, nseg, stride=seg), :] = val[:, k * LANES:(k + 1) * LANES]

    for k in range(nk):
        xre_sc[k] = x_re[:, k * LANES:(k + 1) * LANES]
        xim_sc[k] = x_im[:, k * LANES:(k + 1) * LANES]

    ar8, ai8 = jnp.broadcast_to(ar, (nseg, w)), jnp.broadcast_to(ai, (nseg, w))

    def step(i, carry):
        hr, hi = carry
        nr = ar8 * hr - ai8 * hi + load_rows(xre_sc, i)
        ni = ar8 * hi + ai8 * hr + load_rows(xim_sc, i)
        store_rows(xre_sc, i, nr)
        store_rows(xim_sc, i, ni)
        return nr, ni

    zero = jnp.zeros((nseg, w), F32)
    loc_r, loc_i = lax.fori_loop(0, seg, step, (zero, zero), unroll=2)

    pr, pi = ar, ai
    for _ in range(seg.bit_length() - 1):
        pr, pi = pr * pr - pi * pi, 2.0 * pr * pi
    hr, hi = carry_sc[0:1, :], carry_sc[1:2, :]
    for j in range(nseg):
        cin_sc[0, j:j + 1, :] = hr
        cin_sc[1, j:j + 1, :] = hi
        hr, hi = (pr * hr - pi * hi + loc_r[j:j + 1, :], pr * hi + pi * hr + loc_i[j:j + 1, :])
    carry_sc[0:1, :] = hr
    carry_sc[1:2, :] = hi

    cin_r, cin_i = cin_sc[0], cin_sc[1]

    def fix(i, carry):
        qr, qi = carry
        store_rows(xre_sc, i, load_rows(xre_sc, i) + (qr * cin_r - qi * cin_i))
        store_rows(xim_sc, i, load_rows(xim_sc, i) + (qr * cin_i + qi * cin_r))
        return ar8 * qr - ai8 * qi, ar8 * qi + ai8 * qr

    lax.fori_loop(0, seg, fix, (ar8, ai8), unroll=2)
    h_re = jnp.concatenate([xre_sc[k] for k in range(nk)], axis=1)
    h_im = jnp.concatenate([xim_sc[k] for k in range(nk)], axis=1)
    g = _s5_output(h_re, h_im, u_ref, c_ref, d_ref)
    g_ref[...] = g
    gb_ref[...] = g.astype(BF16)

    @pl.when(tc == pl.num_programs(2) - 1)
    def _final_state():
        hre_ref[...] = hr
        him_ref[...] = hi


def s5_scan(rest, bcat, ccat, coef, ab, dskip, n_seq, t, m_rows, d_ssm):
    ngb, cw, w2 = bcat.shape
    w = w2 // 2
    lt = min(S5_CHUNK, t)
    nt = t // lt
    assert lt % 8 == 0 and (lt // 8) & (lt // 8 - 1) == 0
    par = lambda shape: pl.BlockSpec((None,) + shape, lambda b, g, j: (g, 0, 0))
    row = pl.BlockSpec((lt, cw), lambda b, g, j: (b * nt + j, g))
    st = pl.BlockSpec((None, 1, w), lambda b, g, j: (b, 0, g))
    return pl.pallas_call(
        _s5_scan_kernel,
        grid=(n_seq, ngb, nt),
        in_specs=[row, par((cw, w2)), par((w2, cw)), par((2, w)), par((2, w)), par((1, cw))],
        out_specs=[row, row, st, st],
        out_shape=[jax.ShapeDtypeStruct((m_rows, d_ssm), F32), jax.ShapeDtypeStruct((m_rows, d_ssm), BF16),
                   jax.ShapeDtypeStruct((n_seq, 1, ngb * w), F32), jax.ShapeDtypeStruct((n_seq, 1, ngb * w), F32)],
        scratch_shapes=[pltpu.VMEM((w // LANES, lt, LANES), F32), pltpu.VMEM((w // LANES, lt, LANES), F32),
                        pltpu.VMEM((2, w), F32), pltpu.VMEM((2, 8, w), F32)],
        compiler_params=_cp("parallel", "parallel", "arbitrary"),
    )(rest, bcat, ccat, coef, ab, dskip)


def _s5_step_kernel(n_real, u_ref, b_ref, c_ref, coef_ref, ab_ref, d_ref, h0re_ref, h0im_ref,
                    g_ref, gb_ref, hre_ref, him_ref):
    x_re, x_im = _s5_input(u_ref, b_ref, coef_ref)
    ar, ai = ab_ref[0:1, :], ab_ref[1:2, :]
    h0r, h0i = h0re_ref[...], h0im_ref[...]
    h_re = x_re + ar * h0r - ai * h0i
    h_im = x_im + ar * h0i + ai * h0r
    g = _s5_output(h_re, h_im, u_ref, c_ref, d_ref)
    g_ref[...] = g
    gb_ref[...] = g.astype(BF16)
    hre_ref[...] = h_re[:n_real, :]
    him_ref[...] = h_im[:n_real, :]


def s5_step(rest, row0, rows, n_real, h0_re, h0_im, bcat, ccat, coef, ab, dskip, d_ssm):
    ngb, cw, w2 = bcat.shape
    w = w2 // 2
    rb = row0 // rows
    par = lambda shape: pl.BlockSpec((None,) + shape, lambda g: (g, 0, 0))
    st = pl.BlockSpec((rows, w), lambda g: (0, g))
    sto = pl.BlockSpec((n_real, w), lambda g: (0, g))
    tile = pl.BlockSpec((rows, cw), lambda g: (0, g))
    return pl.pallas_call(
        functools.partial(_s5_step_kernel, n_real),
        grid=(ngb,),
        in_specs=[pl.BlockSpec((rows, cw), lambda g: (rb, g)), par((cw, w2)), par((w2, cw)), par((2, w)),
                  par((2, w)), par((1, cw)), st, st],
        out_specs=[tile, tile, sto, sto],
        out_shape=[jax.ShapeDtypeStruct((rows, d_ssm), F32), jax.ShapeDtypeStruct((rows, d_ssm), BF16),
                   jax.ShapeDtypeStruct((n_real, ngb * w), F32), jax.ShapeDtypeStruct((n_real, ngb * w), F32)],
        compiler_params=_cp("parallel"),
    )(rest, bcat, ccat, coef, ab, dskip, h0_re, h0_im)


def s5_parameters(a_re, a_im, log_dt, b_re, b_im, c_re, c_im, d_skip):
    g, n, ch = b_re.shape
    gb = math.gcd(S5_GROUPS, g)
    ngb = g // gb
    dt = jnp.exp(log_dt)[:, None]
    lr, li = a_re * dt, a_im * dt
    mag = jnp.exp(lr)
    ab_re, ab_im = mag * jnp.cos(li), mag * jnp.sin(li)
    nr, ni = ab_re - 1.0, ab_im
    den = a_re * a_re + a_im * a_im
    coef_re = (nr * a_re + ni * a_im) / den
    coef_im = (ni * a_re - nr * a_im) / den
    blk = lambda x: x.reshape(ngb, 1, gb * n)
    coef = jnp.concatenate([blk(coef_re), blk(coef_im)], axis=1)
    ab = jnp.concatenate([blk(ab_re), blk(ab_im)], axis=1)
    eye = jnp.eye(gb, dtype=F32)

    def bdiag_in(b):
        bb = b.reshape(ngb, gb, n, ch)
        return jnp.einsum('kgnc,gh->kgchn', bb, eye).reshape(ngb, gb * ch, gb * n)

    def bdiag_out(c):
        cc = c.reshape(ngb, gb, ch, n)
        return jnp.einsum('kgcn,gh->kgnhc', cc, eye).reshape(ngb, gb * n, gb * ch)

    bcat = jnp.concatenate([bdiag_in(b_re), bdiag_in(b_im)], axis=2)
    ccat = jnp.concatenate([bdiag_out(c_re), -bdiag_out(c_im)], axis=1)
    return bcat, ccat, coef, ab, d_skip.reshape(ngb, 1, gb * ch)


def _rows_from_chunks(ref, rows):
    ch = ref.shape[0] // rows
    return jnp.concatenate([ref[pl.ds(c, rows, stride=ch), :] for c in range(ch)], axis=1)


def _rows_to_chunks(ref, val):
    rows = val.shape[0]
    ch = ref.shape[0] // rows
    for c in range(ch):
        ref[pl.ds(c, rows, stride=ch), :] = val[:, c * LANES:(c + 1) * LANES]


def _router_kernel(n_groups, per_group, x_ref, g_ref, w_ref, b_ref, hn_ref, eid_ref, wts_ref):
    x = x_ref[...]
    hn = x * lax.rsqrt(jnp.mean(x * x, axis=-1, keepdims=True) + RMS_EPS) * g_ref[...]
    _rows_to_chunks(hn_ref, hn)
    logits = jnp.dot(hn, w_ref[...], preferred_element_type=F32, precision=lax.Precision.HIGHEST) + b_ref[...]
    lane = lax.broadcasted_iota(I32, logits.shape, 1).astype(F32)
    far = float(LANES)
    red_max = lambda v: jnp.max(v, axis=-1, keepdims=True)
    red_min = lambda v: jnp.min(v, axis=-1, keepdims=True)
    gmask = lane < n_groups
    gl = jnp.where(gmask, logits, -jnp.inf)
    gmax = red_max(gl)
    g_top = 1.0 / jnp.sum(jnp.where(gmask, jnp.exp(gl - gmax), 0.0), axis=-1, keepdims=True)
    g_idx = red_min(jnp.where(gl == gmax, lane, far))
    lo = n_groups + g_idx * per_group
    el = jnp.where((lane >= lo) & (lane < lo + per_group), logits, -jnp.inf)
    t1 = red_max(el)
    i1 = red_min(jnp.where(el == t1, lane, far))
    el2 = jnp.where(lane == i1, -jnp.inf, el)
    t2 = red_max(el2)
    i2 = red_min(jnp.where(el2 == t2, lane, far))
    e21 = jnp.exp(t2 - t1)
    w1 = 1.0 / (1.0 + e21)
    w2 = e21 * w1
    eid_ref[...] = jnp.where(lane == 0.0, i1 - n_groups, jnp.where(lane == 1.0, i2 - n_groups, 0.0)).astype(I32)
    wts_ref[...] = jnp.where(lane == 0.0, w1 * g_top, jnp.where(lane == 1.0, w2 * g_top, 0.0))


def moe_route(x, g, w_router, b_router, n_groups, per_group, tr):
    m, d = x.shape
    ch = d // LANES
    rows = lambda wd: pl.BlockSpec((tr, wd), lambda i: (i, 0))
    return pl.pallas_call(
        functools.partial(_router_kernel, n_groups, per_group),
        grid=(m // tr,),
        in_specs=[rows(d), pl.BlockSpec((1, d), lambda i: (0, 0)), pl.BlockSpec((d, LANES), lambda i: (0, 0)),
                  pl.BlockSpec((1, LANES), lambda i: (0, 0))],
        out_specs=[pl.BlockSpec((tr * ch, LANES), lambda i: (i, 0)), rows(LANES), rows(LANES)],
        out_shape=[jax.ShapeDtypeStruct((m * ch, LANES), F32), jax.ShapeDtypeStruct((m, LANES), I32),
                   jax.ShapeDtypeStruct((m, LANES), F32)],
        compiler_params=_cp("parallel"),
    )(x, g.reshape(1, d), w_router, b_router.reshape(1, LANES))


def moe_schedule(eid, n_experts, tile_rows):
    m = eid.shape[0]
    e = eid[:, :TOP_K_IN_GROUP].reshape(-1)
    n_assign = e.shape[0]
    n_tiles = -(-n_assign // tile_rows) + n_experts
    onehot = (e[:, None] == jnp.arange(n_experts, dtype=I32)[None, :]).astype(I32)
    counts = jnp.sum(onehot, axis=0)
    rank = jnp.take_along_axis(jnp.cumsum(onehot, axis=0), e[:, None], axis=1)[:, 0] - 1
    padded = -(-counts // tile_rows) * tile_rows
    ends = jnp.cumsum(padded)
    dest = (ends - padded)[e] + rank
    src_token = jnp.zeros((n_tiles * tile_rows,), I32).at[dest].set(jnp.arange(n_assign, dtype=I32) // TOP_K_IN_GROUP)
    tile_row0 = jnp.arange(n_tiles, dtype=I32) * tile_rows
    tile_valid = (tile_row0 < ends[-1]).astype(I32)
    tile_expert = jnp.sum((ends[None, :] <= jnp.minimum(tile_row0, ends[-1] - 1)[:, None]).astype(I32), axis=1)
    tile_first = jnp.concatenate([jnp.ones((1,), I32), (tile_expert[1:] != tile_expert[:-1]).astype(I32)])
    slot_rows = dest.reshape(m, TOP_K_IN_GROUP).T.reshape(-1)
    tile_read = jnp.arange(n_tiles, dtype=I32) * tile_valid
    return src_token, slot_rows, (tile_expert, tile_first, tile_valid, tile_read)


def _gather_kernel(rows_step, ch, idx_ref, live_ref, src_ref, out_ref, sem):
    step = pl.program_id(0)
    base = step * rows_step

    def copy(r):
        src0 = pl.multiple_of(idx_ref[base + r] * ch, ch)
        dst0 = pl.multiple_of((base + r) * ch, ch)
        return pltpu.make_async_copy(src_ref.at[pl.ds(src0, ch)], out_ref.at[pl.ds(dst0, ch)], sem)

    def issue(r, carry):
        copy(r).start()
        return carry

    def drain(r, carry):
        copy(r).wait()
        return carry

    @pl.when(live_ref[step] == 1)
    def _copy_rows():
        lax.fori_loop(0, rows_step, issue, 0)
        lax.fori_loop(0, rows_step, drain, 0)


def gather_rows(src, idx, step_live, rows_step, ch):
    n = idx.shape[0]
    grid_spec = pltpu.PrefetchScalarGridSpec(
        num_scalar_prefetch=2,
        grid=(n // rows_step,),
        in_specs=[pl.BlockSpec(memory_space=pl.ANY)],
        out_specs=pl.BlockSpec(memory_space=pl.ANY),
        scratch_shapes=[pltpu.SemaphoreType.DMA(())],
    )
    return pl.pallas_call(
        functools.partial(_gather_kernel, rows_step, ch),
        grid_spec=grid_spec,
        out_shape=jax.ShapeDtypeStruct((n * ch, LANES), src.dtype),
        compiler_params=_cp("arbitrary"),
    )(idx, step_live, src)


def _moe_up_kernel(te_ref, tf_ref, tv_ref, tr_ref, x_ref, wg_ref, wu_ref, o_ref, wgb_ref, wub_ref):
    t = pl.program_id(1)

    @pl.when(tf_ref[t] == 1)
    def _cast_weights():
        wgb_ref[...] = wg_ref[...].astype(BF16)
        wub_ref[...] = wu_ref[...].astype(BF16)

    @pl.when(tv_ref[t] == 1)
    def _compute():
        x = _rows_from_chunks(x_ref, o_ref.shape[0]).astype(BF16)
        a = jnp.dot(x, wgb_ref[...], preferred_element_type=F32)
        b = jnp.dot(x, wub_ref[...], preferred_element_type=F32)
        o_ref[...] = (a * jax.nn.sigmoid(a) * b).astype(o_ref.dtype)

    @pl.when(tv_ref[t] == 0)
    def _unused_tile():
        o_ref[...] = jnp.zeros_like(o_ref)


def _moe_up_f32_kernel(te_ref, tf_ref, tv_ref, tr_ref, x_ref, wg_ref, wu_ref, o_ref):
    t = pl.program_id(1)

    @pl.when(tv_ref[t] == 1)
    def _compute():
        x = _rows_from_chunks(x_ref, o_ref.shape[0])
        a = _dot_f32(x, wg_ref[...])
        o_ref[...] = a * jax.nn.sigmoid(a) * _dot_f32(x, wu_ref[...])

    @pl.when(tv_ref[t] == 0)
    def _unused_tile():
        o_ref[...] = jnp.zeros_like(o_ref)


def moe_up(xs, w_gate, w_up, layer, tiles, tile_rows, f32_dots=False):
    d, f = w_gate.shape[-2:]
    ch = d // LANES
    r = xs.shape[0] // ch
    fc = _col_tile(f, 256)
    w_spec = pl.BlockSpec((None, None, d, fc), lambda c, t, te, tf, tv, tr: (layer, te[t], 0, c))
    grid_spec = pltpu.PrefetchScalarGridSpec(
        num_scalar_prefetch=4,
        grid=(f // fc, r // tile_rows),
        in_specs=[pl.BlockSpec((tile_rows * ch, LANES), lambda c, t, te, tf, tv, tr: (tr[t], 0)), w_spec, w_spec],
        out_specs=pl.BlockSpec((tile_rows, fc), lambda c, t, te, tf, tv, tr: (t, c)),
        scratch_shapes=[] if f32_dots else [pltpu.VMEM((d, fc), BF16), pltpu.VMEM((d, fc), BF16)],
    )
    return pl.pallas_call(
        _moe_up_f32_kernel if f32_dots else _moe_up_kernel, grid_spec=grid_spec,
        out_shape=jax.ShapeDtypeStruct((r, f), F32 if f32_dots else BF16),
        compiler_params=_cp("arbitrary", "arbitrary"),
    )(*tiles, xs, w_gate, w_up)


def _moe_down_kernel(te_ref, tf_ref, tv_ref, tr_ref, h_ref, wd_ref, o_ref, wdb_ref):
    t = pl.program_id(0)

    @pl.when(tf_ref[t] == 1)
    def _cast_weights():
        wdb_ref[...] = wd_ref[...].astype(BF16)

    @pl.when(tv_ref[t] == 1)
    def _compute():
        _rows_to_chunks(o_ref, jnp.dot(h_ref[...], wdb_ref[...], preferred_element_type=F32))

    @pl.when(tv_ref[t] == 0)
    def _unused_tile():
        o_ref[...] = jnp.zeros_like(o_ref)


def _moe_down_f32_kernel(te_ref, tf_ref, tv_ref, tr_ref, h_ref, wd_ref, o_ref):
    t = pl.program_id(0)

    @pl.when(tv_ref[t] == 1)
    def _compute():
        _rows_to_chunks(o_ref, _dot_f32(h_ref[...], wd_ref[...]))

    @pl.when(tv_ref[t] == 0)
    def _unused_tile():
        o_ref[...] = jnp.zeros_like(o_ref)


def moe_down(hid, w_down, layer, tiles, tile_rows, f32_dots=False):
    r, f = hid.shape
    d = w_down.shape[-1]
    ch = d // LANES
    grid_spec = pltpu.PrefetchScalarGridSpec(
        num_scalar_prefetch=4,
        grid=(r // tile_rows,),
        in_specs=[pl.BlockSpec((tile_rows, f), lambda t, te, tf, tv, tr: (t, 0)),
                  pl.BlockSpec((None, None, f, d), lambda t, te, tf, tv, tr: (layer, te[t], 0, 0))],
        out_specs=pl.BlockSpec((tile_rows * ch, LANES), lambda t, te, tf, tv, tr: (t, 0)),
        scratch_shapes=[] if f32_dots else [pltpu.VMEM((f, d), BF16)],
    )
    return pl.pallas_call(
        _moe_down_f32_kernel if f32_dots else _moe_down_kernel, grid_spec=grid_spec,
        out_shape=jax.ShapeDtypeStruct((r * ch, LANES), F32),
        compiler_params=_cp("arbitrary"),
    )(*tiles, hid, w_down)


def _combine_kernel(x_ref, y0_ref, y1_ref, w_ref, g_ref, x2_ref, hn_ref):
    w = w_ref[...]
    rows = x_ref.shape[0]
    x2 = (x_ref[...] + w[:, 0:1] * _rows_from_chunks(y0_ref, rows) + w[:, 1:2] * _rows_from_chunks(y1_ref, rows))
    x2_ref[...] = x2
    hn = x2 * lax.rsqrt(jnp.mean(x2 * x2, axis=-1, keepdims=True) + RMS_EPS) * g_ref[...]
    hn_ref[...] = hn.astype(hn_ref.dtype)


def moe_combine(x, y_slots, wts, g, tr, act):
    m, d = x.shape
    nb = m // tr
    ch = d // LANES
    rows = pl.BlockSpec((tr, d), lambda i: (i, 0))
    return pl.pallas_call(
        _combine_kernel,
        grid=(nb,),
        in_specs=[rows, pl.BlockSpec((tr * ch, LANES), lambda i: (i, 0)),
                  pl.BlockSpec((tr * ch, LANES), lambda i: (i + nb, 0)),
                  pl.BlockSpec((tr, LANES), lambda i: (i, 0)), pl.BlockSpec((1, d), lambda i: (0, 0))],
        out_specs=[rows, rows],
        out_shape=[jax.ShapeDtypeStruct((m, d), F32), jax.ShapeDtypeStruct((m, d), act)],
        compiler_params=_cp("parallel"),
    )(x, y_slots, y_slots, wts, g.reshape(1, d))


def kernel(x_prompt, x_sample, cache_k, cache_v, cache_lf, state_ssm_re, state_ssm_im, page_table, p_prompt, p_sample, g_mix, w_in, g_q, g_k, b_f, ssm_a_re, ssm_a_im, ssm_log_dt, ssm_b_re, ssm_b_im, ssm_c_re, ssm_c_im, ssm_d, w_glu, w_br_ssm, w_br_att, w_out, g_ffn, router_group_w, router_group_b, router_expert_w, router_expert_b, w_gate, w_up, w_down, g_ple, w_ple_gate, w_ple_proj):
    nb_p, t, d = x_prompt.shape
    nb_s = x_sample.shape[0]
    assert x_sample.shape[1] == 1
    depth = w_in.shape[0]
    nh = b_f.shape[1]
    da = nh * HEAD_DIM
    d_ssm = ssm_d.shape[1]
    n_grp, n_state, _ = ssm_b_re.shape[1:]
    n_groups = router_group_w.shape[2]
    n_experts = router_expert_w.shape[2]
    per_group = n_experts // n_groups
    ps = cache_k.shape[2]
    n_pages = page_table.shape[1]
    ple = p_prompt.shape[-1]
    assert n_groups + n_experts <= LANES and nh <= LANES

    bt = nb_p * t
    tm = bt // max(1, round(bt / ROW_TILE_TARGET))
    assert bt % tm == 0 and tm % BF16_SUBLANES == 0 and nb_s % 8 == 0
    tn = _col_tile(da, 512)
    tn_m = _col_tile(d, 256)
    cfg_p = (bt, tm, _row_divisor(tm, ELEM_ROWS_MAX), MOE_ROWS, False)
    cfg_s = (nb_s, nb_s, nb_s, 8, True)

    xp = x_prompt.reshape(bt, d)
    xs = x_sample.reshape(nb_s, d)
    pp_all = p_prompt.reshape(depth, bt, ple)
    ps_all = p_sample.reshape(depth, nb_s, ple)

    cache_k2 = cache_k.reshape(depth, -1, ps * nh, HEAD_DIM)
    cache_v2 = cache_v.reshape(depth, -1, ps * nh, HEAD_DIM)
    past_bias = decay_prefix(cache_lf, page_table).reshape(depth, nb_s, n_pages, 1, ps * nh)
    h0_re = state_ssm_re.reshape(depth, nb_s, -1)
    h0_im = state_ssm_im.reshape(depth, nb_s, -1)
    w_router = jnp.concatenate([router_group_w, router_expert_w,
                                jnp.zeros((depth, d, LANES - n_groups - n_experts), F32)], axis=2)
    b_router = jnp.concatenate([router_group_b, router_expert_b,
                                jnp.zeros((depth, LANES - n_groups - n_experts), F32)], axis=1)
    rest_col0 = 3 * da + nh
    n_rest = w_in.shape[2] - rest_col0
    assert n_rest == d_ssm + 2 * d and da % tn == 0 and d_ssm % tn == 0 and d % tn == 0

    ident = lambda accs, ex: accs
    glu = lambda accs, ex: [ex[0] * jax.nn.sigmoid(accs[0])]
    gated = lambda accs, ex: [jax.nn.sigmoid(ex[0]) * accs[0] + jax.nn.sigmoid(ex[1]) * accs[1]]
    resid = lambda accs, ex: [ex[0] + accs[0]]
    ple_gate = lambda accs, ex: [ex[0] + jax.nn.sigmoid(accs[0]) * accs[1]]

    def project(x, i, w_rest, cfg):
        m, tmc, tr, _, hi = cfg
        act = F32 if hi else BF16
        h = rmsnorm_rows(x, g_mix[i], tr, act)
        (qkv,) = fused_matmul([(h, 0, w_in, i, 0)], [], ident, [F32], m, 3 * da, tmc, tn, hi)
        (f_logit,) = fused_matmul([(h, 0, w_in, i, 3 * da // LANES)], [], ident, [F32], m, LANES, tmc, LANES, hi)
        (rest,) = fused_matmul([(h, 0, w_rest, 0, 0)], [], ident, [F32], m, n_rest, tmc, tn, hi)
        b_f_pad = jnp.concatenate([b_f[i], jnp.zeros((LANES - nh,), F32)]).reshape(1, LANES)
        return (qkv, rest) + tuple(qk_prepare(qkv, f_logit, g_q[i], g_k[i], b_f_pad, tr, nh, act))

    def mix_and_ffn(x, p_all, att, g_f, g_act, rest, i, cfg):
        m, tmc, tr, moe_rows, hi = cfg
        act = F32 if hi else BF16
        (s_out,) = fused_matmul([(g_act, 0, w_glu, i, 0)], [(g_f, 0)], glu, [act], m, d_ssm, tmc, tn, hi)
        (merged,) = fused_matmul([(s_out, 0, w_br_ssm, i, 0), (att, 0, w_br_att, i, 0)],
                                 [(rest, d_ssm // tn_m), (rest, (d_ssm + d) // tn_m)], gated, [act],
                                 m, d, tmc, tn_m, hi)
        (x,) = fused_matmul([(merged, 0, w_out, i, 0)], [(x, 0)], resid, [F32], m, d, tmc, tn, hi)
        hn, eid, wts = moe_route(x, g_ffn[i], w_router[i], b_router[i], n_groups, per_group, tr)
        src_token, slot_rows, tiles = moe_schedule(eid, n_experts, moe_rows)
        x_sorted = gather_rows(hn, src_token, tiles[2], moe_rows, d // LANES)
        hid = moe_up(x_sorted, w_gate, w_up, i, tiles, moe_rows, hi)
        y_sorted = moe_down(hid, w_down, i, tiles, moe_rows, hi)
        slot_step = math.gcd(DMA_ROWS, slot_rows.shape[0])
        y_slots = gather_rows(y_sorted, slot_rows, jnp.ones((slot_rows.shape[0] // slot_step,), I32), slot_step,
                              d // LANES)
        x, hn2 = moe_combine(x, y_slots, wts, g_ple[i], tr, act)
        (x,) = fused_matmul([(hn2, 0, w_ple_gate, i, 0), (p_all, i, w_ple_proj, i, 0)], [(x, 0)], ple_gate,
                            [F32], m, d, tmc, tn_m, hi)
        return x

    outs = [[] for _ in range(10)]
    for i in range(depth):
        w_rest = w_in[i][:, rest_col0:]
        qkv_p, rest_p, qn, kn_p, kb, vb, lf_p = project(xp, i, w_rest, cfg_p)
        qkv_s, rest_s, qn_s, kn_s, _, vb_s, lf_s = project(xs, i, w_rest, cfg_s)

        c = cumsum_time(lf_p, nb_p, t, min(256, t))
        tq = min(ATT_Q, t)
        ck_row = c[:, :nh].reshape(nb_p, t, nh).transpose(0, 2, 1).reshape(nb_p, nh, t // tq, 1, tq)
        att_p = prompt_attention(qn, kb, vb, c, ck_row, nb_p, t, nh, bt)
        tok3 = lambda a: a.reshape(nb_s, nh, HEAD_DIM)
        lf_new = jnp.tile(lf_s[:, :nh], (1, ps)).reshape(nb_s, 1, ps * nh)
        att_s = decode_attention(tok3(qn_s), tok3(kn_s), tok3(vb_s), lf_new, past_bias[i], cache_k2, cache_v2,
                                 page_table, i).reshape(nb_s, da)

        bcat, ccat, coef, ab, dskip = s5_parameters(ssm_a_re[i], ssm_a_im[i], ssm_log_dt[i], ssm_b_re[i],
                                                    ssm_b_im[i], ssm_c_re[i], ssm_c_im[i], ssm_d[i])
        gp_f, gp_b, hp_re, hp_im = s5_scan(rest_p, bcat.astype(BF16), ccat.astype(BF16), coef, ab, dskip,
                                           nb_p, t, bt, d_ssm)
        gs_f, _, hs_re, hs_im = s5_step(rest_s, 0, nb_s, nb_s, h0_re[i], h0_im[i], bcat, ccat, coef, ab, dskip,
                                        d_ssm)

        xp = mix_and_ffn(xp, pp_all, att_p, gp_f, gp_b, rest_p, i, cfg_p)
        xs = mix_and_ffn(xs, ps_all, att_s, gs_f, gs_f, rest_s, i, cfg_s)

        per_layer = (kn_p.reshape(nb_p, t, nh, HEAD_DIM), qkv_p[:, 2 * da:].reshape(nb_p, t, nh, HEAD_DIM),
                     lf_p[:, :nh].reshape(nb_p, t, nh),
                     hp_re.reshape(nb_p, n_grp, n_state), hp_im.reshape(nb_p, n_grp, n_state),
                     kn_s.reshape(nb_s, 1, nh, HEAD_DIM), qkv_s[:, 2 * da:].reshape(nb_s, 1, nh, HEAD_DIM),
                     lf_s[:, :nh].reshape(nb_s, 1, nh),
                     hs_re.reshape(nb_s, n_grp, n_state), hs_im.reshape(nb_s, n_grp, n_state))
        for lst, val in zip(outs, per_layer):
            lst.append(val)

    return (xp.reshape(nb_p, t, d), xs.reshape(nb_s, 1, d)) + tuple(jnp.stack(lst) for lst in outs)
```

```python
import functools
import math

import jax
import jax.numpy as jnp
from jax import lax
from jax.experimental import pallas as pl
from jax.experimental.pallas import tpu as pltpu

F32 = jnp.float32
BF16 = jnp.bfloat16
I32 = jnp.int32

RMS_EPS = 1e-6
NEG_INF = -1e30
HEAD_DIM = 128
LANES = 128
BF16_SUBLANES = 16
TOP_K_IN_GROUP = 2
V7X_VMEM_LIMIT_BYTES = 58 * 1024 * 1024

ROW_TILE_TARGET = 1024
ELEM_ROWS_MAX = 256
MOE_ROWS = 256
DMA_ROWS = 256
ATT_Q = 512
DEC_PAGES = 4
PREP_PAGES = 8
S5_CHUNK = 256
S5_GROUPS = 16


def _cp(*sem):
    return pltpu.CompilerParams(dimension_semantics=sem, vmem_limit_bytes=V7X_VMEM_LIMIT_BYTES)


def _round_up(x, m):
    return -(-x // m) * m


def _row_divisor(tm, max_rows):
    best = BF16_SUBLANES
    for r in range(BF16_SUBLANES, max_rows + 1, BF16_SUBLANES):
        if tm % r == 0:
            best = r
    return best


def _col_tile(n, target):
    t = min(n, target)
    while n % t:
        t //= 2
    return t


def _split3(x):
    hi = x.astype(BF16)
    r1 = x - hi.astype(F32)
    mid = r1.astype(BF16)
    lo = (r1 - mid.astype(F32)).astype(BF16)
    return hi, mid, lo


def _split2(x):
    hi = x.astype(BF16)
    return hi, (x - hi.astype(F32)).astype(BF16)


def _dot_exact_lhs01(tri, x):
    hi, mid, lo = _split3(x)
    d = functools.partial(jnp.dot, preferred_element_type=F32)
    return d(tri, hi) + d(tri, mid) + d(tri, lo)


def _rmsnorm_kernel(x_ref, g_ref, o_ref):
    x = x_ref[...]
    y = x * lax.rsqrt(jnp.mean(x * x, axis=-1, keepdims=True) + RMS_EPS)
    o_ref[...] = (y * g_ref[...]).astype(o_ref.dtype)


def rmsnorm_rows(x, g, tr, out_dtype):
    m, d = x.shape
    return pl.pallas_call(
        _rmsnorm_kernel,
        grid=(m // tr,),
        in_specs=[pl.BlockSpec((tr, d), lambda i: (i, 0)), pl.BlockSpec((1, d), lambda i: (0, 0))],
        out_specs=pl.BlockSpec((tr, d), lambda i: (i, 0)),
        out_shape=jax.ShapeDtypeStruct((m, d), out_dtype),
        compiler_params=_cp("parallel"),
    )(x, g.reshape(1, d))


def _dot_f32(a, b):
    return jnp.dot(a, b, preferred_element_type=F32, precision=lax.Precision.HIGHEST)


def _fused_matmul_f32_kernel(n_dots, n_extra, epilogue, *refs):
    a_refs = refs[:n_dots]
    w_refs = refs[n_dots:2 * n_dots]
    e_refs = refs[2 * n_dots:2 * n_dots + n_extra]
    o_refs = refs[2 * n_dots + n_extra:]
    accs = [_dot_f32(a_ref[...], w_ref[...]) for a_ref, w_ref in zip(a_refs, w_refs)]
    outs = epilogue(accs, [e_ref[...] for e_ref in e_refs])
    for o_ref, o in zip(o_refs, outs):
        o_ref[...] = o.astype(o_ref.dtype)


def _fused_matmul_kernel(n_dots, n_extra, n_out, epilogue, cast_rows, *refs):
    a_refs = refs[:n_dots]
    w_refs = refs[n_dots:2 * n_dots]
    e_refs = refs[2 * n_dots:2 * n_dots + n_extra]
    o_refs = refs[2 * n_dots + n_extra:2 * n_dots + n_extra + n_out]
    wb_refs = refs[2 * n_dots + n_extra + n_out:]

    @pl.when(pl.program_id(1) == 0)
    def _cast_weights():
        for w_ref, wb_ref in zip(w_refs, wb_refs):
            k = w_ref.shape[0]
            for r0 in range(0, k, cast_rows):
                r1 = min(k, r0 + cast_rows)
                wb_ref[r0:r1, :] = w_ref[r0:r1, :].astype(BF16)

    accs = [jnp.dot(a_ref[...].astype(BF16), wb_ref[...], preferred_element_type=F32)
            for a_ref, wb_ref in zip(a_refs, wb_refs)]
    outs = epilogue(accs, [e_ref[...] for e_ref in e_refs])
    for o_ref, o in zip(o_refs, outs):
        o_ref[...] = o.astype(o_ref.dtype)


def fused_matmul(dots, extras, epilogue, out_dtypes, m, n, tm, tn, f32_dots=False):
    in_specs, args, scratch = [], [], []
    for a, a_lead, _, _, _ in dots:
        k = a.shape[-1]
        if a.ndim == 3:
            in_specs.append(pl.BlockSpec((None, tm, k), lambda j, i, l=a_lead: (l, i, 0)))
        else:
            in_specs.append(pl.BlockSpec((tm, k), lambda j, i: (i, 0)))
        args.append(a)
    for _, _, w, w_lead, w_cb in dots:
        k = w.shape[-2]
        if w.ndim == 3:
            in_specs.append(pl.BlockSpec((None, k, tn), lambda j, i, l=w_lead, c=w_cb: (l, 0, j + c)))
        else:
            in_specs.append(pl.BlockSpec((k, tn), lambda j, i, c=w_cb: (0, j + c)))
        args.append(w)
        if not f32_dots:
            scratch.append(pltpu.VMEM((k, tn), BF16))
    for e, e_cb in extras:
        in_specs.append(pl.BlockSpec((tm, tn), lambda j, i, c=e_cb: (i, j + c)))
        args.append(e)
    n_out = len(out_dtypes)
    if f32_dots:
        kern = functools.partial(_fused_matmul_f32_kernel, len(dots), len(extras), epilogue)
    else:
        kern = functools.partial(_fused_matmul_kernel, len(dots), len(extras), n_out, epilogue, 512)
    outs = pl.pallas_call(
        kern,
        grid=(n // tn, m // tm),
        in_specs=in_specs,
        out_specs=[pl.BlockSpec((tm, tn), lambda j, i: (i, j)) for _ in out_dtypes],
        out_shape=[jax.ShapeDtypeStruct((m, n), dt) for dt in out_dtypes],
        scratch_shapes=scratch,
        compiler_params=_cp("arbitrary", "arbitrary"),
    )(*args)
    return outs


def _qkprep_kernel(n_heads, q_ref, k_ref, v_ref, f_ref, gq_ref, gk_ref, bf_ref,
                   qn_ref, kn_ref, kb_ref, vb_ref, lf_ref):
    gq = gq_ref[...]
    gk = gk_ref[...]
    for h in range(n_heads):
        sl = slice(h * HEAD_DIM, (h + 1) * HEAD_DIM)
        q = q_ref[:, sl]
        qn = q * lax.rsqrt(jnp.mean(q * q, axis=-1, keepdims=True) + RMS_EPS) * gq
        qn_ref[:, sl] = qn.astype(qn_ref.dtype)
        k = k_ref[:, sl]
        kn = k * lax.rsqrt(jnp.mean(k * k, axis=-1, keepdims=True) + RMS_EPS) * gk
        kn_ref[:, sl] = kn
        kb_ref[:, sl] = kn.astype(kb_ref.dtype)
    vb_ref[...] = v_ref[...].astype(vb_ref.dtype)
    z = f_ref[...] + bf_ref[...]
    lf_ref[...] = jnp.minimum(z, 0.0) - jnp.log1p(jnp.exp(-jnp.abs(z)))


def qk_prepare(qkv, f, g_q, g_k, b_f_pad, tr, n_heads, act):
    m = qkv.shape[0]
    da = n_heads * HEAD_DIM
    row = lambda c: pl.BlockSpec((tr, da), lambda i, c=c: (i, c))
    vec = pl.BlockSpec((1, LANES), lambda i: (0, 0))
    big = lambda dt: jax.ShapeDtypeStruct((m, da), dt)
    return pl.pallas_call(
        functools.partial(_qkprep_kernel, n_heads),
        grid=(m // tr,),
        in_specs=[row(0), row(1), row(2), pl.BlockSpec((tr, LANES), lambda i: (i, 0)), vec, vec, vec],
        out_specs=[row(0), row(0), row(0), row(0), pl.BlockSpec((tr, LANES), lambda i: (i, 0))],
        out_shape=[big(act), big(F32), big(act), big(act), jax.ShapeDtypeStruct((m, LANES), F32)],
        compiler_params=_cp("parallel"),
    )(qkv, qkv, qkv, f, g_q.reshape(1, HEAD_DIM), g_k.reshape(1, HEAD_DIM), b_f_pad)


def _cumsum_kernel(x_ref, o_ref, carry_ref):
    @pl.when(pl.program_id(1) == 0)
    def _init():
        carry_ref[...] = jnp.zeros_like(carry_ref)

    lc = x_ref.shape[0]
    r = lax.broadcasted_iota(I32, (lc, lc), 0)
    c = lax.broadcasted_iota(I32, (lc, lc), 1)
    tri = jnp.where(c <= r, 1.0, 0.0).astype(BF16)
    out = _dot_exact_lhs01(tri, x_ref[...]) + carry_ref[...]
    o_ref[...] = out
    carry_ref[...] = out[lc - 1:lc, :]


def cumsum_time(x, n_seq, t, lc):
    m = x.shape[0]
    nb = t // lc
    return pl.pallas_call(
        _cumsum_kernel,
        grid=(n_seq, nb),
        in_specs=[pl.BlockSpec((lc, LANES), lambda b, j: (b * nb + j, 0))],
        out_specs=pl.BlockSpec((lc, LANES), lambda b, j: (b * nb + j, 0)),
        out_shape=jax.ShapeDtypeStruct((n_seq * t, LANES), F32),
        scratch_shapes=[pltpu.VMEM((1, LANES), F32)],
        compiler_params=_cp("parallel", "arbitrary"),
    )(x)


def _flash_kernel(scale, tq, q_ref, k_ref, v_ref, c_ref, ck_ref, o_ref, m_sc, l_sc, acc_sc):
    h = pl.program_id(1)
    qi = pl.program_id(2)
    lane = lax.broadcasted_iota(I32, (tq, LANES), 1)
    cq = jnp.sum(jnp.where(lane == h, c_ref[...], 0.0), axis=-1, keepdims=True)
    m_sc[...] = jnp.full_like(m_sc, NEG_INF)
    l_sc[...] = jnp.zeros_like(l_sc)
    acc_sc[...] = jnp.zeros_like(acc_sc)
    q = q_ref[...]

    def block(ki, diagonal):
        off = pl.multiple_of(ki * tq, tq)
        k = k_ref[pl.ds(off, tq), :]
        v = v_ref[pl.ds(off, tq), :]
        s = lax.dot_general(q, k, (((1,), (1,)), ((), ())), preferred_element_type=F32) * scale
        s = s + cq - ck_ref[ki]
        if diagonal:
            causal = (lax.broadcasted_iota(I32, (tq, tq), 1) <= lax.broadcasted_iota(I32, (tq, tq), 0))
            s = jnp.where(causal, s, NEG_INF)
        m_prev = m_sc[...]
        m_new = jnp.maximum(m_prev, jnp.max(s, axis=-1, keepdims=True))
        alpha = jnp.exp(m_prev - m_new)
        p = jnp.exp(s - m_new)
        l_sc[...] = alpha * l_sc[...] + jnp.sum(p, axis=-1, keepdims=True)
        acc_sc[...] = alpha * acc_sc[...] + jnp.dot(p.astype(BF16), v, preferred_element_type=F32)
        m_sc[...] = m_new

    def body(ki, carry):
        block(ki, False)
        return carry

    lax.fori_loop(0, qi, body, 0)
    block(qi, True)
    o_ref[...] = (acc_sc[...] / l_sc[...]).astype(o_ref.dtype)


def prompt_attention(qn, kb, vb, c, ck_row, n_seq, t, n_heads, m_rows):
    tq = min(ATT_Q, t)
    nq = t // tq
    kv_spec = pl.BlockSpec((t, HEAD_DIM), lambda b, h, i: (b, h))
    return pl.pallas_call(
        functools.partial(_flash_kernel, HEAD_DIM ** -0.5, tq),
        grid=(n_seq, n_heads, nq),
        in_specs=[
            pl.BlockSpec((tq, HEAD_DIM), lambda b, h, i: (b * nq + i, h)),
            kv_spec, kv_spec,
            pl.BlockSpec((tq, LANES), lambda b, h, i: (b * nq + i, 0)),
            pl.BlockSpec((None, None, nq, 1, tq), lambda b, h, i: (b, h, 0, 0, 0)),
        ],
        out_specs=pl.BlockSpec((tq, HEAD_DIM), lambda b, h, i: (b * nq + i, h)),
        out_shape=jax.ShapeDtypeStruct((m_rows, n_heads * HEAD_DIM), BF16),
        scratch_shapes=[pltpu.VMEM((tq, 1), F32), pltpu.VMEM((tq, 1), F32), pltpu.VMEM((tq, HEAD_DIM), F32)],
        compiler_params=_cp("parallel", "parallel", "arbitrary"),
    )(qn, kb, vb, c, ck_row)


def _decay_prefix_kernel(n_pages_step, pt_ref, *refs):
    lf_refs = refs[:n_pages_step]
    o_ref = refs[n_pages_step]
    carry_ref = refs[n_pages_step + 1]

    @pl.when(pl.program_id(1) == 0)
    def _init():
        carry_ref[...] = jnp.zeros_like(carry_ref)

    ps = lf_refs[0].shape[0]
    r = lax.broadcasted_iota(I32, (ps, ps), 0)
    c = lax.broadcasted_iota(I32, (ps, ps), 1)
    upper = jnp.where(c > r, 1.0, 0.0).astype(BF16)
    carry = carry_ref[...]
    for idx in range(n_pages_step):
        x = lf_refs[idx][...]
        suffix = _dot_exact_lhs01(upper, x) + carry
        slot = n_pages_step - 1 - idx
        o_ref[slot * ps:(slot + 1) * ps, :] = suffix
        carry = suffix[0:1, :] + x[0:1, :]
    carry_ref[...] = carry


def decay_prefix(cache_lf, page_table):
    depth, _, ps, nh = cache_lf.shape
    nb, n_pages = page_table.shape
    pp = math.gcd(PREP_PAGES, n_pages)
    steps = n_pages // pp

    def lf_spec(idx):
        def imap(lb, j, pt):
            page = n_pages - 1 - (j * pp + idx)
            return (lb // nb, pt[(lb % nb) * n_pages + page], 0, 0)
        return pl.BlockSpec((None, None, ps, nh), imap)

    grid_spec = pltpu.PrefetchScalarGridSpec(
        num_scalar_prefetch=1,
        grid=(depth * nb, steps),
        in_specs=[lf_spec(idx) for idx in range(pp)],
        out_specs=pl.BlockSpec((None, None, pp * ps, nh),
                               lambda lb, j, pt: (lb // nb, lb % nb, steps - 1 - j, 0)),
        scratch_shapes=[pltpu.VMEM((1, nh), F32)],
    )
    return pl.pallas_call(
        functools.partial(_decay_prefix_kernel, pp),
        grid_spec=grid_spec,
        out_shape=jax.ShapeDtypeStruct((depth, nb, n_pages * ps, nh), F32),
        compiler_params=_cp("parallel", "arbitrary"),
    )(page_table.reshape(-1), *([cache_lf] * pp))


def _decode_kernel(scale, n_pages_step, n_heads, pt_ref, q_ref, kn_ref, vn_ref, lfn_ref, bias_ref, *refs):
    k_refs = refs[:n_pages_step]
    v_refs = refs[n_pages_step:2 * n_pages_step]
    o_ref = refs[2 * n_pages_step]
    m_sc, l_sc, acc_sc = refs[2 * n_pages_step + 1:]
    j = pl.program_id(1)

    @pl.when(j == 0)
    def _init():
        m_sc[...] = jnp.full_like(m_sc, NEG_INF)
        l_sc[...] = jnp.zeros_like(l_sc)
        acc_sc[...] = jnp.zeros_like(acc_sc)

    q = q_ref[...]
    q_hi, q_lo = _split2(q)
    width = k_refs[0].shape[0]
    own = (lax.broadcasted_iota(I32, (n_heads, width), 1) % n_heads
           == lax.broadcasted_iota(I32, (n_heads, width), 0))
    lfn = lfn_ref[...]
    qk = lambda a, b: lax.dot_general(a, b, (((1,), (1,)), ((), ())), preferred_element_type=F32)
    pv = functools.partial(jnp.dot, preferred_element_type=F32)
    for idx in range(n_pages_step):
        k_hi, k_lo = _split2(k_refs[idx][...])
        s = (qk(q_hi, k_hi) + qk(q_lo, k_hi) + qk(q_hi, k_lo)) * scale
        s = jnp.where(own, s + bias_ref[idx] + lfn, NEG_INF)
        m_prev = m_sc[...]
        m_new = jnp.maximum(m_prev, jnp.max(s, axis=-1, keepdims=True))
        alpha = jnp.exp(m_prev - m_new)
        p = jnp.exp(s - m_new)
        l_sc[...] = alpha * l_sc[...] + jnp.sum(p, axis=-1, keepdims=True)
        p_hi, p_lo = _split2(p)
        v_hi, v_lo = _split2(v_refs[idx][...])
        acc_sc[...] = alpha * acc_sc[...] + pv(p_hi, v_hi) + pv(p_lo, v_hi) + pv(p_hi, v_lo)
        m_sc[...] = m_new

    @pl.when(j == pl.num_programs(1) - 1)
    def _finish():
        s = jnp.sum(q * kn_ref[...], axis=-1, keepdims=True) * scale
        m_prev = m_sc[...]
        m_new = jnp.maximum(m_prev, s)
        alpha = jnp.exp(m_prev - m_new)
        p = jnp.exp(s - m_new)
        l = alpha * l_sc[...] + p
        acc = alpha * acc_sc[...] + p * vn_ref[...]
        o_ref[...] = (acc / l).astype(o_ref.dtype)


def decode_attention(q, k_new, v_new, lf_new_tiled, bias, cache_k, cache_v, page_table, layer):
    nb, nh, _ = q.shape
    n_pages = page_table.shape[1]
    width = cache_k.shape[2]
    pp = math.gcd(DEC_PAGES, n_pages)
    steps = n_pages // pp

    def page_spec(idx):
        return pl.BlockSpec((None, None, width, HEAD_DIM),
                            lambda b, j, pt: (layer, pt[b * n_pages + j * pp + idx], 0, 0))

    tok = pl.BlockSpec((None, nh, HEAD_DIM), lambda b, j, pt: (b, 0, 0))
    grid_spec = pltpu.PrefetchScalarGridSpec(
        num_scalar_prefetch=1,
        grid=(nb, steps),
        in_specs=[tok, tok, tok,
                  pl.BlockSpec((None, 1, width), lambda b, j, pt: (b, 0, 0)),
                  pl.BlockSpec((None, pp, 1, width), lambda b, j, pt: (b, j, 0, 0))]
                 + [page_spec(idx) for idx in range(pp)] * 2,
        out_specs=pl.BlockSpec((None, nh, HEAD_DIM), lambda b, j, pt: (b, 0, 0)),
        scratch_shapes=[pltpu.VMEM((nh, 1), F32), pltpu.VMEM((nh, 1), F32), pltpu.VMEM((nh, HEAD_DIM), F32)],
    )
    return pl.pallas_call(
        functools.partial(_decode_kernel, HEAD_DIM ** -0.5, pp, nh),
        grid_spec=grid_spec,
        out_shape=jax.ShapeDtypeStruct((nb, nh, HEAD_DIM), F32),
        compiler_params=_cp("parallel", "arbitrary"),
    )(page_table.reshape(-1), q, k_new, v_new, lf_new_tiled, bias, *([cache_k] * pp), *([cache_v] * pp))


def _s5_dot(a, b_ref_slice):
    if b_ref_slice.dtype == F32:
        return _dot_f32(a, b_ref_slice)
    return jnp.dot(a.astype(BF16), b_ref_slice, preferred_element_type=F32)


def _s5_input(u_ref, b_ref, coef_ref):
    w = coef_ref.shape[-1]
    bu = _s5_dot(u_ref[...], b_ref[...])
    bu_re, bu_im = bu[:, :w], bu[:, w:]
    cr, ci = coef_ref[0:1, :], coef_ref[1:2, :]
    return cr * bu_re - ci * bu_im, cr * bu_im + ci * bu_re


def _s5_output(h_re, h_im, u_ref, c_ref, d_ref):
    w = h_re.shape[-1]
    y = _s5_dot(h_re, c_ref[:w, :]) + _s5_dot(h_im, c_ref[w:, :]) + d_ref[...] * u_ref[...]
    return jax.nn.gelu(y)


def _s5_scan_kernel(u_ref, b_ref, c_ref, coef_ref, ab_ref, d_ref, g_ref, gb_ref, hre_ref, him_ref,
                    xre_sc, xim_sc, carry_sc, cin_sc):
    tc = pl.program_id(2)

    @pl.when(tc == 0)
    def _init():
        carry_sc[...] = jnp.zeros_like(carry_sc)

    lt, w = xre_sc.shape
    nseg = 8
    seg = lt // nseg
    p_row = lax.broadcasted_iota(I32, (lt, lt), 0)
    t_col = lax.broadcasted_iota(I32, (lt, lt), 1)
    perm = jnp.where(t_col == (p_row % nseg) * seg + p_row // nseg, 1.0, 0.0).astype(BF16)
    unperm = jnp.where(p_row == (t_col % nseg) * seg + t_col // nseg, 1.0, 0.0).astype(BF16)
    u_perm = jnp.dot(perm, u_ref[...].astype(BF16), preferred_element_type=F32).astype(BF16)
    x_re, x_im = _s5_input(u_perm, b_ref, coef_ref)
    xre_sc[...] = x_re
    xim_sc[...] = x_im
    ar, ai = ab_ref[0:1, :], ab_ref[1:2, :]
    seg_rows = lambda i: pl.ds(pl.multiple_of(i * nseg, nseg), nseg)

    def load_rows(sc, i):
        return sc[seg_rows(i), :]

    def store_rows(sc, i, val):
        sc[seg_rows(i), :] = val

    ar8, ai8 = jnp.broadcast_to(ar, (nseg, w)), jnp.broadcast_to(ai, (nseg, w))

    def step(i, carry):
        hr, hi = carry
        nr = ar8 * hr - ai8 * hi + load_rows(xre_sc, i)
        ni = ar8 * hi + ai8 * hr + load_rows(xim_sc, i)
        store_rows(xre_sc, i, nr)
        store_rows(xim_sc, i, ni)
        return nr, ni

    zero = jnp.zeros((nseg, w), F32)
    loc_r, loc_i = lax.fori_loop(0, seg, step, (zero, zero), unroll=2)

    pr, pi = ar, ai
    for _ in range(seg.bit_length() - 1):
        pr, pi = pr * pr - pi * pi, 2.0 * pr * pi
    hr, hi = carry_sc[0:1, :], carry_sc[1:2, :]
    for j in range(nseg):
        cin_sc[0, j:j + 1, :] = hr
        cin_sc[1, j:j + 1, :] = hi
        hr, hi = (pr * hr - pi * hi + loc_r[j:j + 1, :], pr * hi + pi * hr + loc_i[j:j + 1, :])
    carry_sc[0:1, :] = hr
    carry_sc[1:2, :] = hi

    cin_r, cin_i = cin_sc[0], cin_sc[1]

    def fix(i, carry):
        qr, qi = carry
        store_rows(xre_sc, i, load_rows(xre_sc, i) + (qr * cin_r - qi * cin_i))
        store_rows(xim_sc, i, load_rows(xim_sc, i) + (qr * cin_i + qi * cin_r))
        return ar8 * qr - ai8 * qi, ar8 * qi + ai8 * qr

    lax.fori_loop(0, seg, fix, (ar8, ai8), unroll=2)
    y_perm = _s5_dot(xre_sc[...], c_ref[:w, :]) + _s5_dot(xim_sc[...], c_ref[w:, :])
    g = jax.nn.gelu(_dot_exact_lhs01(unperm, y_perm) + d_ref[...] * u_ref[...])
    g_ref[...] = g
    gb_ref[...] = g.astype(BF16)

    @pl.when(tc == pl.num_programs(2) - 1)
    def _final_state():
        hre_ref[...] = hr
        him_ref[...] = hi


def s5_scan(rest, bcat, ccat, coef, ab, dskip, n_seq, t, m_rows, d_ssm):
    ngb, cw, w2 = bcat.shape
    w = w2 // 2
    lt = min(S5_CHUNK, t)
    nt = t // lt
    assert lt % 8 == 0 and (lt // 8) & (lt // 8 - 1) == 0
    par = lambda shape: pl.BlockSpec((None,) + shape, lambda b, g, j: (g, 0, 0))
    row = pl.BlockSpec((lt, cw), lambda b, g, j: (b * nt + j, g))
    st = pl.BlockSpec((None, 1, w), lambda b, g, j: (b, 0, g))
    return pl.pallas_call(
        _s5_scan_kernel,
        grid=(n_seq, ngb, nt),
        in_specs=[row, par((cw, w2)), par((w2, cw)), par((2, w)), par((2, w)), par((1, cw))],
        out_specs=[row, row, st, st],
        out_shape=[jax.ShapeDtypeStruct((m_rows, d_ssm), F32), jax.ShapeDtypeStruct((m_rows, d_ssm), BF16),
                   jax.ShapeDtypeStruct((n_seq, 1, ngb * w), F32), jax.ShapeDtypeStruct((n_seq, 1, ngb * w), F32)],
        scratch_shapes=[pltpu.VMEM((lt, w), F32), pltpu.VMEM((lt, w), F32),
                        pltpu.VMEM((2, w), F32), pltpu.VMEM((2, 8, w), F32)],
        compiler_params=_cp("parallel", "parallel", "arbitrary"),
    )(rest, bcat, ccat, coef, ab, dskip)


def _s5_step_kernel(n_real, u_ref, b_ref, c_ref, coef_ref, ab_ref, d_ref, h0re_ref, h0im_ref,
                    g_ref, gb_ref, hre_ref, him_ref):
    x_re, x_im = _s5_input(u_ref, b_ref, coef_ref)
    ar, ai = ab_ref[0:1, :], ab_ref[1:2, :]
    h0r, h0i = h0re_ref[...], h0im_ref[...]
    h_re = x_re + ar * h0r - ai * h0i
    h_im = x_im + ar * h0i + ai * h0r
    g = _s5_output(h_re, h_im, u_ref, c_ref, d_ref)
    g_ref[...] = g
    gb_ref[...] = g.astype(BF16)
    hre_ref[...] = h_re[:n_real, :]
    him_ref[...] = h_im[:n_real, :]


def s5_step(rest, row0, rows, n_real, h0_re, h0_im, bcat, ccat, coef, ab, dskip, d_ssm):
    ngb, cw, w2 = bcat.shape
    w = w2 // 2
    rb = row0 // rows
    par = lambda shape: pl.BlockSpec((None,) + shape, lambda g: (g, 0, 0))
    st = pl.BlockSpec((rows, w), lambda g: (0, g))
    sto = pl.BlockSpec((n_real, w), lambda g: (0, g))
    tile = pl.BlockSpec((rows, cw), lambda g: (0, g))
    return pl.pallas_call(
        functools.partial(_s5_step_kernel, n_real),
        grid=(ngb,),
        in_specs=[pl.BlockSpec((rows, cw), lambda g: (rb, g)), par((cw, w2)), par((w2, cw)), par((2, w)),
                  par((2, w)), par((1, cw)), st, st],
        out_specs=[tile, tile, sto, sto],
        out_shape=[jax.ShapeDtypeStruct((rows, d_ssm), F32), jax.ShapeDtypeStruct((rows, d_ssm), BF16),
                   jax.ShapeDtypeStruct((n_real, ngb * w), F32), jax.ShapeDtypeStruct((n_real, ngb * w), F32)],
        compiler_params=_cp("parallel"),
    )(rest, bcat, ccat, coef, ab, dskip, h0_re, h0_im)


def s5_parameters(a_re, a_im, log_dt, b_re, b_im, c_re, c_im, d_skip):
    g, n, ch = b_re.shape
    gb = math.gcd(S5_GROUPS, g)
    ngb = g // gb
    dt = jnp.exp(log_dt)[:, None]
    lr, li = a_re * dt, a_im * dt
    mag = jnp.exp(lr)
    ab_re, ab_im = mag * jnp.cos(li), mag * jnp.sin(li)
    nr, ni = ab_re - 1.0, ab_im
    den = a_re * a_re + a_im * a_im
    coef_re = (nr * a_re + ni * a_im) / den
    coef_im = (ni * a_re - nr * a_im) / den
    blk = lambda x: x.reshape(ngb, 1, gb * n)
    coef = jnp.concatenate([blk(coef_re), blk(coef_im)], axis=1)
    ab = jnp.concatenate([blk(ab_re), blk(ab_im)], axis=1)
    eye = jnp.eye(gb, dtype=F32)

    def bdiag_in(b):
        bb = b.reshape(ngb, gb, n, ch)
        return jnp.einsum('kgnc,gh->kgchn', bb, eye).reshape(ngb, gb * ch, gb * n)

    def bdiag_out(c):
        cc = c.reshape(ngb, gb, ch, n)
        return jnp.einsum('kgcn,gh->kgnhc', cc, eye).reshape(ngb, gb * n, gb * ch)

    bcat = jnp.concatenate([bdiag_in(b_re), bdiag_in(b_im)], axis=2)
    ccat = jnp.concatenate([bdiag_out(c_re), -bdiag_out(c_im)], axis=1)
    return bcat, ccat, coef, ab, d_skip.reshape(ngb, 1, gb * ch)


def _rows_from_chunks(ref, rows):
    ch = ref.shape[0] // rows
    return jnp.concatenate([ref[pl.ds(c, rows, stride=ch), :] for c in range(ch)], axis=1)


def _rows_to_chunks(ref, val):
    rows = val.shape[0]
    ch = ref.shape[0] // rows
    for c in range(ch):
        ref[pl.ds(c, rows, stride=ch), :] = val[:, c * LANES:(c + 1) * LANES]


def _router_kernel(n_groups, per_group, x_ref, g_ref, w_ref, b_ref, hn_ref, eid_ref, wts_ref):
    x = x_ref[...]
    hn = x * lax.rsqrt(jnp.mean(x * x, axis=-1, keepdims=True) + RMS_EPS) * g_ref[...]
    _rows_to_chunks(hn_ref, hn)
    logits = jnp.dot(hn, w_ref[...], preferred_element_type=F32, precision=lax.Precision.HIGHEST) + b_ref[...]
    lane = lax.broadcasted_iota(I32, logits.shape, 1).astype(F32)
    far = float(LANES)
    red_max = lambda v: jnp.max(v, axis=-1, keepdims=True)
    red_min = lambda v: jnp.min(v, axis=-1, keepdims=True)
    gmask = lane < n_groups
    gl = jnp.where(gmask, logits, -jnp.inf)
    gmax = red_max(gl)
    g_top = 1.0 / jnp.sum(jnp.where(gmask, jnp.exp(gl - gmax), 0.0), axis=-1, keepdims=True)
    g_idx = red_min(jnp.where(gl == gmax, lane, far))
    lo = n_groups + g_idx * per_group
    el = jnp.where((lane >= lo) & (lane < lo + per_group), logits, -jnp.inf)
    t1 = red_max(el)
    i1 = red_min(jnp.where(el == t1, lane, far))
    el2 = jnp.where(lane == i1, -jnp.inf, el)
    t2 = red_max(el2)
    i2 = red_min(jnp.where(el2 == t2, lane, far))
    e21 = jnp.exp(t2 - t1)
    w1 = 1.0 / (1.0 + e21)
    w2 = e21 * w1
    eid_ref[...] = jnp.where(lane == 0.0, i1 - n_groups, jnp.where(lane == 1.0, i2 - n_groups, 0.0)).astype(I32)
    wts_ref[...] = jnp.where(lane == 0.0, w1 * g_top, jnp.where(lane == 1.0, w2 * g_top, 0.0))


def moe_route(x, g, w_router, b_router, n_groups, per_group, tr):
    m, d = x.shape
    ch = d // LANES
    rows = lambda wd: pl.BlockSpec((tr, wd), lambda i: (i, 0))
    return pl.pallas_call(
        functools.partial(_router_kernel, n_groups, per_group),
        grid=(m // tr,),
        in_specs=[rows(d), pl.BlockSpec((1, d), lambda i: (0, 0)), pl.BlockSpec((d, LANES), lambda i: (0, 0)),
                  pl.BlockSpec((1, LANES), lambda i: (0, 0))],
        out_specs=[pl.BlockSpec((tr * ch, LANES), lambda i: (i, 0)), rows(LANES), rows(LANES)],
        out_shape=[jax.ShapeDtypeStruct((m * ch, LANES), F32), jax.ShapeDtypeStruct((m, LANES), I32),
                   jax.ShapeDtypeStruct((m, LANES), F32)],
        compiler_params=_cp("parallel"),
    )(x, g.reshape(1, d), w_router, b_router.reshape(1, LANES))


def moe_schedule(eid, n_experts, tile_rows):
    m = eid.shape[0]
    e = eid[:, :TOP_K_IN_GROUP].reshape(-1)
    n_assign = e.shape[0]
    n_tiles = -(-n_assign // tile_rows) + n_experts
    onehot = (e[:, None] == jnp.arange(n_experts, dtype=I32)[None, :]).astype(I32)
    counts = jnp.sum(onehot, axis=0)
    rank = jnp.take_along_axis(jnp.cumsum(onehot, axis=0), e[:, None], axis=1)[:, 0] - 1
    padded = -(-counts // tile_rows) * tile_rows
    ends = jnp.cumsum(padded)
    dest = (ends - padded)[e] + rank
    src_token = jnp.zeros((n_tiles * tile_rows,), I32).at[dest].set(jnp.arange(n_assign, dtype=I32) // TOP_K_IN_GROUP)
    tile_row0 = jnp.arange(n_tiles, dtype=I32) * tile_rows
    tile_valid = (tile_row0 < ends[-1]).astype(I32)
    tile_expert = jnp.sum((ends[None, :] <= jnp.minimum(tile_row0, ends[-1] - 1)[:, None]).astype(I32), axis=1)
    tile_first = jnp.concatenate([jnp.ones((1,), I32), (tile_expert[1:] != tile_expert[:-1]).astype(I32)])
    slot_rows = dest.reshape(m, TOP_K_IN_GROUP).T.reshape(-1)
    tile_read = jnp.arange(n_tiles, dtype=I32) * tile_valid
    return src_token, slot_rows, (tile_expert, tile_first, tile_valid, tile_read)


def _gather_kernel(rows_step, ch, idx_ref, live_ref, src_ref, out_ref, sem):
    step = pl.program_id(0)
    base = step * rows_step

    def copy(r):
        src0 = pl.multiple_of(idx_ref[base + r] * ch, ch)
        dst0 = pl.multiple_of(r * ch, ch)
        return pltpu.make_async_copy(src_ref.at[pl.ds(src0, ch)], out_ref.at[pl.ds(dst0, ch)], sem)

    def issue(r, carry):
        copy(r).start()
        return carry

    def drain(r, carry):
        copy(r).wait()
        return carry

    @pl.when(live_ref[step] == 1)
    def _copy_rows():
        lax.fori_loop(0, rows_step, issue, 0)
        lax.fori_loop(0, rows_step, drain, 0)

    @pl.when(live_ref[step] == 0)
    def _unused_rows():
        out_ref[...] = jnp.zeros_like(out_ref)


def gather_rows(src, idx, step_live, rows_step, ch):
    n = idx.shape[0]
    grid_spec = pltpu.PrefetchScalarGridSpec(
        num_scalar_prefetch=2,
        grid=(n // rows_step,),
        in_specs=[pl.BlockSpec(memory_space=pl.ANY)],
        out_specs=pl.BlockSpec((rows_step * ch, LANES), lambda s, idx_ref, live_ref: (s, 0)),
        scratch_shapes=[pltpu.SemaphoreType.DMA(())],
    )
    return pl.pallas_call(
        functools.partial(_gather_kernel, rows_step, ch),
        grid_spec=grid_spec,
        out_shape=jax.ShapeDtypeStruct((n * ch, LANES), src.dtype),
        compiler_params=_cp("arbitrary"),
    )(idx, step_live, src)


def _moe_up_kernel(te_ref, tf_ref, tv_ref, tr_ref, x_ref, wg_ref, wu_ref, o_ref, wgb_ref, wub_ref):
    t = pl.program_id(1)

    @pl.when(tf_ref[t] == 1)
    def _cast_weights():
        wgb_ref[...] = wg_ref[...].astype(BF16)
        wub_ref[...] = wu_ref[...].astype(BF16)

    @pl.when(tv_ref[t] == 1)
    def _compute():
        x = _rows_from_chunks(x_ref, o_ref.shape[0]).astype(BF16)
        a = jnp.dot(x, wgb_ref[...], preferred_element_type=F32)
        b = jnp.dot(x, wub_ref[...], preferred_element_type=F32)
        o_ref[...] = (a * jax.nn.sigmoid(a) * b).astype(o_ref.dtype)

    @pl.when(tv_ref[t] == 0)
    def _unused_tile():
        o_ref[...] = jnp.zeros_like(o_ref)


def _moe_up_f32_kernel(te_ref, tf_ref, tv_ref, tr_ref, x_ref, wg_ref, wu_ref, o_ref):
    t = pl.program_id(1)

    @pl.when(tv_ref[t] == 1)
    def _compute():
        x = _rows_from_chunks(x_ref, o_ref.shape[0])
        a = _dot_f32(x, wg_ref[...])
        o_ref[...] = a * jax.nn.sigmoid(a) * _dot_f32(x, wu_ref[...])

    @pl.when(tv_ref[t] == 0)
    def _unused_tile():
        o_ref[...] = jnp.zeros_like(o_ref)


def moe_up(xs, w_gate, w_up, layer, tiles, tile_rows, f32_dots=False):
    d, f = w_gate.shape[-2:]
    ch = d // LANES
    r = xs.shape[0] // ch
    fc = _col_tile(f, 256)
    w_spec = pl.BlockSpec((None, None, d, fc), lambda c, t, te, tf, tv, tr: (layer, te[t], 0, c))
    grid_spec = pltpu.PrefetchScalarGridSpec(
        num_scalar_prefetch=4,
        grid=(f // fc, r // tile_rows),
        in_specs=[pl.BlockSpec((tile_rows * ch, LANES), lambda c, t, te, tf, tv, tr: (tr[t], 0)), w_spec, w_spec],
        out_specs=pl.BlockSpec((tile_rows, fc), lambda c, t, te, tf, tv, tr: (t, c)),
        scratch_shapes=[] if f32_dots else [pltpu.VMEM((d, fc), BF16), pltpu.VMEM((d, fc), BF16)],
    )
    return pl.pallas_call(
        _moe_up_f32_kernel if f32_dots else _moe_up_kernel, grid_spec=grid_spec,
        out_shape=jax.ShapeDtypeStruct((r, f), F32 if f32_dots else BF16),
        compiler_params=_cp("arbitrary", "arbitrary"),
    )(*tiles, xs, w_gate, w_up)


def _moe_down_kernel(te_ref, tf_ref, tv_ref, tr_ref, h_ref, wd_ref, o_ref, wdb_ref):
    t = pl.program_id(0)

    @pl.when(tf_ref[t] == 1)
    def _cast_weights():
        wdb_ref[...] = wd_ref[...].astype(BF16)

    @pl.when(tv_ref[t] == 1)
    def _compute():
        _rows_to_chunks(o_ref, jnp.dot(h_ref[...], wdb_ref[...], preferred_element_type=F32))

    @pl.when(tv_ref[t] == 0)
    def _unused_tile():
        o_ref[...] = jnp.zeros_like(o_ref)


def _moe_down_f32_kernel(te_ref, tf_ref, tv_ref, tr_ref, h_ref, wd_ref, o_ref):
    t = pl.program_id(0)

    @pl.when(tv_ref[t] == 1)
    def _compute():
        _rows_to_chunks(o_ref, _dot_f32(h_ref[...], wd_ref[...]))

    @pl.when(tv_ref[t] == 0)
    def _unused_tile():
        o_ref[...] = jnp.zeros_like(o_ref)


def moe_down(hid, w_down, layer, tiles, tile_rows, f32_dots=False):
    r, f = hid.shape
    d = w_down.shape[-1]
    ch = d // LANES
    grid_spec = pltpu.PrefetchScalarGridSpec(
        num_scalar_prefetch=4,
        grid=(r // tile_rows,),
        in_specs=[pl.BlockSpec((tile_rows, f), lambda t, te, tf, tv, tr: (t, 0)),
                  pl.BlockSpec((None, None, f, d), lambda t, te, tf, tv, tr: (layer, te[t], 0, 0))],
        out_specs=pl.BlockSpec((tile_rows * ch, LANES), lambda t, te, tf, tv, tr: (t, 0)),
        scratch_shapes=[] if f32_dots else [pltpu.VMEM((f, d), BF16)],
    )
    return pl.pallas_call(
        _moe_down_f32_kernel if f32_dots else _moe_down_kernel, grid_spec=grid_spec,
        out_shape=jax.ShapeDtypeStruct((r * ch, LANES), F32),
        compiler_params=_cp("arbitrary"),
    )(*tiles, hid, w_down)


def _combine_kernel(x_ref, y0_ref, y1_ref, w_ref, g_ref, x2_ref, hn_ref):
    w = w_ref[...]
    rows = x_ref.shape[0]
    x2 = (x_ref[...] + w[:, 0:1] * _rows_from_chunks(y0_ref, rows) + w[:, 1:2] * _rows_from_chunks(y1_ref, rows))
    x2_ref[...] = x2
    hn = x2 * lax.rsqrt(jnp.mean(x2 * x2, axis=-1, keepdims=True) + RMS_EPS) * g_ref[...]
    hn_ref[...] = hn.astype(hn_ref.dtype)


def moe_combine(x, y_slots, wts, g, tr, act):
    m, d = x.shape
    nb = m // tr
    ch = d // LANES
    rows = pl.BlockSpec((tr, d), lambda i: (i, 0))
    return pl.pallas_call(
        _combine_kernel,
        grid=(nb,),
        in_specs=[rows, pl.BlockSpec((tr * ch, LANES), lambda i: (i, 0)),
                  pl.BlockSpec((tr * ch, LANES), lambda i: (i + nb, 0)),
                  pl.BlockSpec((tr, LANES), lambda i: (i, 0)), pl.BlockSpec((1, d), lambda i: (0, 0))],
        out_specs=[rows, rows],
        out_shape=[jax.ShapeDtypeStruct((m, d), F32), jax.ShapeDtypeStruct((m, d), act)],
        compiler_params=_cp("parallel"),
    )(x, y_slots, y_slots, wts, g.reshape(1, d))


def kernel(x_prompt, x_sample, cache_k, cache_v, cache_lf, state_ssm_re, state_ssm_im, page_table, p_prompt, p_sample, g_mix, w_in, g_q, g_k, b_f, ssm_a_re, ssm_a_im, ssm_log_dt, ssm_b_re, ssm_b_im, ssm_c_re, ssm_c_im, ssm_d, w_glu, w_br_ssm, w_br_att, w_out, g_ffn, router_group_w, router_group_b, router_expert_w, router_expert_b, w_gate, w_up, w_down, g_ple, w_ple_gate, w_ple_proj):
    nb_p, t, d = x_prompt.shape
    nb_s = x_sample.shape[0]
    assert x_sample.shape[1] == 1
    depth = w_in.shape[0]
    nh = b_f.shape[1]
    da = nh * HEAD_DIM
    d_ssm = ssm_d.shape[1]
    n_grp, n_state, _ = ssm_b_re.shape[1:]
    n_groups = router_group_w.shape[2]
    n_experts = router_expert_w.shape[2]
    per_group = n_experts // n_groups
    ps = cache_k.shape[2]
    n_pages = page_table.shape[1]
    ple = p_prompt.shape[-1]
    assert n_groups + n_experts <= LANES and nh <= LANES

    bt = nb_p * t
    tm = bt // max(1, round(bt / ROW_TILE_TARGET))
    assert bt % tm == 0 and tm % BF16_SUBLANES == 0 and nb_s % 8 == 0
    tn = _col_tile(da, 512)
    tn_m = _col_tile(d, 256)
    cfg_p = (bt, tm, _row_divisor(tm, ELEM_ROWS_MAX), MOE_ROWS, False)
    cfg_s = (nb_s, nb_s, nb_s, 8, True)

    xp = x_prompt.reshape(bt, d)
    xs = x_sample.reshape(nb_s, d)
    pp_all = p_prompt.reshape(depth, bt, ple)
    ps_all = p_sample.reshape(depth, nb_s, ple)

    cache_k2 = cache_k.reshape(depth, -1, ps * nh, HEAD_DIM)
    cache_v2 = cache_v.reshape(depth, -1, ps * nh, HEAD_DIM)
    past_bias = decay_prefix(cache_lf, page_table).reshape(depth, nb_s, n_pages, 1, ps * nh)
    h0_re = state_ssm_re.reshape(depth, nb_s, -1)
    h0_im = state_ssm_im.reshape(depth, nb_s, -1)
    w_router = jnp.concatenate([router_group_w, router_expert_w,
                                jnp.zeros((depth, d, LANES - n_groups - n_experts), F32)], axis=2)
    b_router = jnp.concatenate([router_group_b, router_expert_b,
                                jnp.zeros((depth, LANES - n_groups - n_experts), F32)], axis=1)
    rest_col0 = 3 * da + nh
    n_rest = w_in.shape[2] - rest_col0
    assert n_rest == d_ssm + 2 * d and da % tn == 0 and d_ssm % tn == 0 and d % tn == 0

    ident = lambda accs, ex: accs
    glu = lambda accs, ex: [ex[0] * jax.nn.sigmoid(accs[0])]
    gated = lambda accs, ex: [jax.nn.sigmoid(ex[0]) * accs[0] + jax.nn.sigmoid(ex[1]) * accs[1]]
    resid = lambda accs, ex: [ex[0] + accs[0]]
    ple_gate = lambda accs, ex: [ex[0] + jax.nn.sigmoid(accs[0]) * accs[1]]

    def project(x, i, w_rest, cfg):
        m, tmc, tr, _, hi = cfg
        act = F32 if hi else BF16
        h = rmsnorm_rows(x, g_mix[i], tr, act)
        (qkv,) = fused_matmul([(h, 0, w_in, i, 0)], [], ident, [F32], m, 3 * da, tmc, tn, hi)
        (f_logit,) = fused_matmul([(h, 0, w_in, i, 3 * da // LANES)], [], ident, [F32], m, LANES, tmc, LANES, hi)
        (rest,) = fused_matmul([(h, 0, w_rest, 0, 0)], [], ident, [F32], m, n_rest, tmc, tn, hi)
        b_f_pad = jnp.concatenate([b_f[i], jnp.zeros((LANES - nh,), F32)]).reshape(1, LANES)
        return (qkv, rest) + tuple(qk_prepare(qkv, f_logit, g_q[i], g_k[i], b_f_pad, tr, nh, act))

    def mix_and_ffn(x, p_all, att, g_f, g_act, rest, i, cfg):
        m, tmc, tr, moe_rows, hi = cfg
        act = F32 if hi else BF16
        (s_out,) = fused_matmul([(g_act, 0, w_glu, i, 0)], [(g_f, 0)], glu, [act], m, d_ssm, tmc, tn, hi)
        (merged,) = fused_matmul([(s_out, 0, w_br_ssm, i, 0), (att, 0, w_br_att, i, 0)],
                                 [(rest, d_ssm // tn_m), (rest, (d_ssm + d) // tn_m)], gated, [act],
                                 m, d, tmc, tn_m, hi)
        (x,) = fused_matmul([(merged, 0, w_out, i, 0)], [(x, 0)], resid, [F32], m, d, tmc, tn, hi)
        hn, eid, wts = moe_route(x, g_ffn[i], w_router[i], b_router[i], n_groups, per_group, tr)
        src_token, slot_rows, tiles = moe_schedule(eid, n_experts, moe_rows)
        x_sorted = gather_rows(hn, src_token, tiles[2], moe_rows, d // LANES)
        hid = moe_up(x_sorted, w_gate, w_up, i, tiles, moe_rows, hi)
        y_sorted = moe_down(hid, w_down, i, tiles, moe_rows, hi)
        slot_step = math.gcd(DMA_ROWS, slot_rows.shape[0])
        y_slots = gather_rows(y_sorted, slot_rows, jnp.ones((slot_rows.shape[0] // slot_step,), I32), slot_step,
                              d // LANES)
        x, hn2 = moe_combine(x, y_slots, wts, g_ple[i], tr, act)
        (x,) = fused_matmul([(hn2, 0, w_ple_gate, i, 0), (p_all, i, w_ple_proj, i, 0)], [(x, 0)], ple_gate,
                            [F32], m, d, tmc, tn_m, hi)
        return x

    outs = [[] for _ in range(10)]
    for i in range(depth):
        w_rest = w_in[i][:, rest_col0:]
        qkv_p, rest_p, qn, kn_p, kb, vb, lf_p = project(xp, i, w_rest, cfg_p)
        qkv_s, rest_s, qn_s, kn_s, _, vb_s, lf_s = project(xs, i, w_rest, cfg_s)

        c = cumsum_time(lf_p, nb_p, t, min(256, t))
        tq = min(ATT_Q, t)
        ck_row = c[:, :nh].reshape(nb_p, t, nh).transpose(0, 2, 1).reshape(nb_p, nh, t // tq, 1, tq)
        att_p = prompt_attention(qn, kb, vb, c, ck_row, nb_p, t, nh, bt)
        tok3 = lambda a: a.reshape(nb_s, nh, HEAD_DIM)
        lf_new = jnp.tile(lf_s[:, :nh], (1, ps)).reshape(nb_s, 1, ps * nh)
        att_s = decode_attention(tok3(qn_s), tok3(kn_s), tok3(vb_s), lf_new, past_bias[i], cache_k2, cache_v2,
                                 page_table, i).reshape(nb_s, da)

        bcat, ccat, coef, ab, dskip = s5_parameters(ssm_a_re[i], ssm_a_im[i], ssm_log_dt[i], ssm_b_re[i],
                                                    ssm_b_im[i], ssm_c_re[i], ssm_c_im[i], ssm_d[i])
        gp_f, gp_b, hp_re, hp_im = s5_scan(rest_p, bcat.astype(BF16), ccat.astype(BF16), coef, ab, dskip,
                                           nb_p, t, bt, d_ssm)
        gs_f, _, hs_re, hs_im = s5_step(rest_s, 0, nb_s, nb_s, h0_re[i], h0_im[i], bcat, ccat, coef, ab, dskip,
                                        d_ssm)

        xp = mix_and_ffn(xp, pp_all, att_p, gp_f, gp_b, rest_p, i, cfg_p)
        xs = mix_and_ffn(xs, ps_all, att_s, gs_f, gs_f, rest_s, i, cfg_s)

        per_layer = (kn_p.reshape(nb_p, t, nh, HEAD_DIM), qkv_p[:, 2 * da:].reshape(nb_p, t, nh, HEAD_DIM),
                     lf_p[:, :nh].reshape(nb_p, t, nh),
                     hp_re.reshape(nb_p, n_grp, n_state), hp_im.reshape(nb_p, n_grp, n_state),
                     kn_s.reshape(nb_s, 1, nh, HEAD_DIM), qkv_s[:, 2 * da:].reshape(nb_s, 1, nh, HEAD_DIM),
                     lf_s[:, :nh].reshape(nb_s, 1, nh),
                     hs_re.reshape(nb_s, n_grp, n_state), hs_im.reshape(nb_s, n_grp, n_state))
        for lst, val in zip(outs, per_layer):
            lst.append(val)

    return (xp.reshape(nb_p, t, d), xs.reshape(nb_s, 1, d)) + tuple(jnp.stack(lst) for lst in outs)
```

```python
import functools
import math

import jax
import jax.numpy as jnp
from jax import lax
from jax.experimental import pallas as pl
from jax.experimental.pallas import tpu as pltpu

F32 = jnp.float32
BF16 = jnp.bfloat16
I32 = jnp.int32

RMS_EPS = 1e-6
NEG_INF = -1e30
HEAD_DIM = 128
LANES = 128
BF16_SUBLANES = 16
TOP_K_IN_GROUP = 2
V7X_VMEM_LIMIT_BYTES = 58 * 1024 * 1024

ROW_TILE_TARGET = 1024
ELEM_ROWS_MAX = 256
MOE_ROWS = 256
DMA_ROWS = 256
ATT_Q = 512
DEC_PAGES = 4
PREP_PAGES = 8
S5_CHUNK = 256
S5_GROUPS = 16


def _cp(*sem):
    return pltpu.CompilerParams(dimension_semantics=sem, vmem_limit_bytes=V7X_VMEM_LIMIT_BYTES)


def _round_up(x, m):
    return -(-x // m) * m


def _row_divisor(tm, max_rows):
    best = BF16_SUBLANES
    for r in range(BF16_SUBLANES, max_rows + 1, BF16_SUBLANES):
        if tm % r == 0:
            best = r
    return best


def _col_tile(n, target):
    t = min(n, target)
    while n % t:
        t //= 2
    return t


def _split3(x):
    hi = x.astype(BF16)
    r1 = x - hi.astype(F32)
    mid = r1.astype(BF16)
    lo = (r1 - mid.astype(F32)).astype(BF16)
    return hi, mid, lo


def _split2(x):
    hi = x.astype(BF16)
    return hi, (x - hi.astype(F32)).astype(BF16)


def _dot_exact_lhs01(tri, x):
    hi, mid, lo = _split3(x)
    d = functools.partial(jnp.dot, preferred_element_type=F32)
    return d(tri, hi) + d(tri, mid) + d(tri, lo)


def _rmsnorm_kernel(x_ref, g_ref, o_ref):
    x = x_ref[...]
    y = x * lax.rsqrt(jnp.mean(x * x, axis=-1, keepdims=True) + RMS_EPS)
    o_ref[...] = (y * g_ref[...]).astype(o_ref.dtype)


def rmsnorm_rows(x, g, tr, out_dtype):
    m, d = x.shape
    return pl.pallas_call(
        _rmsnorm_kernel,
        grid=(m // tr,),
        in_specs=[pl.BlockSpec((tr, d), lambda i: (i, 0)), pl.BlockSpec((1, d), lambda i: (0, 0))],
        out_specs=pl.BlockSpec((tr, d), lambda i: (i, 0)),
        out_shape=jax.ShapeDtypeStruct((m, d), out_dtype),
        compiler_params=_cp("parallel"),
    )(x, g.reshape(1, d))


def _dot_f32(a, b):
    return jnp.dot(a, b, preferred_element_type=F32, precision=lax.Precision.HIGHEST)


def _shifted_rows(w_ref, wn_ref, shift, r0, r1):
    if not shift:
        return w_ref[r0:r1, :]
    return jnp.concatenate([w_ref[r0:r1, shift:], wn_ref[r0:r1, :shift]], axis=1)


def _fused_matmul_f32_kernel(n_dots, n_extra, epilogue, shift, *refs):
    a_refs = refs[:n_dots]
    w_refs = refs[n_dots:2 * n_dots]
    refs = refs[2 * n_dots:]
    wn_ref = refs[0] if shift else None
    refs = refs[1:] if shift else refs
    e_refs = refs[:n_extra]
    o_refs = refs[n_extra:]
    accs = [_dot_f32(a_ref[...], _shifted_rows(w_ref, wn_ref, shift, 0, w_ref.shape[0]))
            for a_ref, w_ref in zip(a_refs, w_refs)]
    outs = epilogue(accs, [e_ref[...] for e_ref in e_refs])
    for o_ref, o in zip(o_refs, outs):
        o_ref[...] = o.astype(o_ref.dtype)


def _fused_matmul_kernel(n_dots, n_extra, n_out, epilogue, cast_rows, shift, *refs):
    a_refs = refs[:n_dots]
    w_refs = refs[n_dots:2 * n_dots]
    refs = refs[2 * n_dots:]
    wn_ref = refs[0] if shift else None
    refs = refs[1:] if shift else refs
    e_refs = refs[:n_extra]
    o_refs = refs[n_extra:n_extra + n_out]
    wb_refs = refs[n_extra + n_out:]

    @pl.when(pl.program_id(1) == 0)
    def _cast_weights():
        for w_ref, wb_ref in zip(w_refs, wb_refs):
            k = w_ref.shape[0]
            for r0 in range(0, k, cast_rows):
                r1 = min(k, r0 + cast_rows)
                wb_ref[r0:r1, :] = _shifted_rows(w_ref, wn_ref, shift, r0, r1).astype(BF16)

    accs = [jnp.dot(a_ref[...].astype(BF16), wb_ref[...], preferred_element_type=F32)
            for a_ref, wb_ref in zip(a_refs, wb_refs)]
    outs = epilogue(accs, [e_ref[...] for e_ref in e_refs])
    for o_ref, o in zip(o_refs, outs):
        o_ref[...] = o.astype(o_ref.dtype)


def fused_matmul(dots, extras, epilogue, out_dtypes, m, n, tm, tn, f32_dots=False, shift=0):
    assert not shift or (len(dots) == 1 and dots[0][2].ndim == 3 and shift < LANES)
    in_specs, args, scratch = [], [], []
    for a, a_lead, _, _, _ in dots:
        k = a.shape[-1]
        if a.ndim == 3:
            in_specs.append(pl.BlockSpec((None, tm, k), lambda j, i, l=a_lead: (l, i, 0)))
        else:
            in_specs.append(pl.BlockSpec((tm, k), lambda j, i: (i, 0)))
        args.append(a)
    for _, _, w, w_lead, w_cb in dots:
        k = w.shape[-2]
        if w.ndim == 3:
            in_specs.append(pl.BlockSpec((None, k, tn), lambda j, i, l=w_lead, c=w_cb: (l, 0, j + c)))
        else:
            in_specs.append(pl.BlockSpec((k, tn), lambda j, i, c=w_cb: (0, j + c)))
        args.append(w)
        if not f32_dots:
            scratch.append(pltpu.VMEM((k, tn), BF16))
    if shift:
        _, _, w, w_lead, w_cb = dots[0]
        per = tn // LANES
        in_specs.append(pl.BlockSpec((None, w.shape[-2], LANES),
                                     lambda j, i, l=w_lead, c=w_cb: (l, 0, (j + c + 1) * per)))
        args.append(w)
    for e, e_cb in extras:
        in_specs.append(pl.BlockSpec((tm, tn), lambda j, i, c=e_cb: (i, j + c)))
        args.append(e)
    n_out = len(out_dtypes)
    if f32_dots:
        kern = functools.partial(_fused_matmul_f32_kernel, len(dots), len(extras), epilogue, shift)
    else:
        kern = functools.partial(_fused_matmul_kernel, len(dots), len(extras), n_out, epilogue, 512, shift)
    outs = pl.pallas_call(
        kern,
        grid=(n // tn, m // tm),
        in_specs=in_specs,
        out_specs=[pl.BlockSpec((tm, tn), lambda j, i: (i, j)) for _ in out_dtypes],
        out_shape=[jax.ShapeDtypeStruct((m, n), dt) for dt in out_dtypes],
        scratch_shapes=scratch,
        compiler_params=_cp("arbitrary", "arbitrary"),
    )(*args)
    return outs


def _qkprep_kernel(n_heads, q_ref, k_ref, v_ref, f_ref, gq_ref, gk_ref, bf_ref,
                   qn_ref, kn_ref, kb_ref, vb_ref, lf_ref):
    gq = gq_ref[...]
    gk = gk_ref[...]
    for h in range(n_heads):
        sl = slice(h * HEAD_DIM, (h + 1) * HEAD_DIM)
        q = q_ref[:, sl]
        qn = q * lax.rsqrt(jnp.mean(q * q, axis=-1, keepdims=True) + RMS_EPS) * gq
        qn_ref[:, sl] = qn.astype(qn_ref.dtype)
        k = k_ref[:, sl]
        kn = k * lax.rsqrt(jnp.mean(k * k, axis=-1, keepdims=True) + RMS_EPS) * gk
        kn_ref[:, sl] = kn
        kb_ref[:, sl] = kn.astype(kb_ref.dtype)
    vb_ref[...] = v_ref[...].astype(vb_ref.dtype)
    z = f_ref[...] + bf_ref[...]
    lf_ref[...] = jnp.minimum(z, 0.0) - jnp.log1p(jnp.exp(-jnp.abs(z)))


def qk_prepare(qkv, f, g_q, g_k, b_f_pad, tr, n_heads, act):
    m = qkv.shape[0]
    da = n_heads * HEAD_DIM
    row = lambda c: pl.BlockSpec((tr, da), lambda i, c=c: (i, c))
    vec = pl.BlockSpec((1, LANES), lambda i: (0, 0))
    big = lambda dt: jax.ShapeDtypeStruct((m, da), dt)
    return pl.pallas_call(
        functools.partial(_qkprep_kernel, n_heads),
        grid=(m // tr,),
        in_specs=[row(0), row(1), row(2), pl.BlockSpec((tr, LANES), lambda i: (i, 0)), vec, vec, vec],
        out_specs=[row(0), row(0), row(0), row(0), pl.BlockSpec((tr, LANES), lambda i: (i, 0))],
        out_shape=[big(act), big(F32), big(act), big(act), jax.ShapeDtypeStruct((m, LANES), F32)],
        compiler_params=_cp("parallel"),
    )(qkv, qkv, qkv, f, g_q.reshape(1, HEAD_DIM), g_k.reshape(1, HEAD_DIM), b_f_pad)


def _cumsum_kernel(x_ref, o_ref, carry_ref):
    @pl.when(pl.program_id(1) == 0)
    def _init():
        carry_ref[...] = jnp.zeros_like(carry_ref)

    lc = x_ref.shape[0]
    r = lax.broadcasted_iota(I32, (lc, lc), 0)
    c = lax.broadcasted_iota(I32, (lc, lc), 1)
    tri = jnp.where(c <= r, 1.0, 0.0).astype(BF16)
    out = _dot_exact_lhs01(tri, x_ref[...]) + carry_ref[...]
    o_ref[...] = out
    carry_ref[...] = out[lc - 1:lc, :]


def cumsum_time(x, n_seq, t, lc):
    m = x.shape[0]
    nb = t // lc
    return pl.pallas_call(
        _cumsum_kernel,
        grid=(n_seq, nb),
        in_specs=[pl.BlockSpec((lc, LANES), lambda b, j: (b * nb + j, 0))],
        out_specs=pl.BlockSpec((lc, LANES), lambda b, j: (b * nb + j, 0)),
        out_shape=jax.ShapeDtypeStruct((n_seq * t, LANES), F32),
        scratch_shapes=[pltpu.VMEM((1, LANES), F32)],
        compiler_params=_cp("parallel", "arbitrary"),
    )(x)


def _flash_kernel(scale, tq, hp, q_ref, k_ref, v_ref, c_ref, ck_ref, o_ref, m_sc, l_sc, acc_sc):
    qi = pl.program_id(2)
    lane = lax.broadcasted_iota(I32, (tq, LANES), 1)
    c_blk = c_ref[...]
    cols = [slice(j * HEAD_DIM, (j + 1) * HEAD_DIM) for j in range(hp)]
    cq = [jnp.sum(jnp.where(lane == pl.program_id(1) * hp + j, c_blk, 0.0), axis=-1, keepdims=True)
          for j in range(hp)]
    m_sc[...] = jnp.full_like(m_sc, NEG_INF)
    l_sc[...] = jnp.zeros_like(l_sc)
    acc_sc[...] = jnp.zeros_like(acc_sc)

    def block(ki, diagonal):
        off = pl.multiple_of(ki * tq, tq)
        for j in range(hp):
            k = k_ref[pl.ds(off, tq), cols[j]]
            v = v_ref[pl.ds(off, tq), cols[j]]
            s = lax.dot_general(q_ref[:, cols[j]], k, (((1,), (1,)), ((), ())),
                                preferred_element_type=F32) * scale
            s = s + cq[j] - ck_ref[j, ki]
            if diagonal:
                causal = (lax.broadcasted_iota(I32, (tq, tq), 1) <= lax.broadcasted_iota(I32, (tq, tq), 0))
                s = jnp.where(causal, s, NEG_INF)
            m_prev = m_sc[j]
            m_new = jnp.maximum(m_prev, jnp.max(s, axis=-1, keepdims=True))
            alpha = jnp.exp(m_prev - m_new)
            p = jnp.exp(s - m_new)
            l_sc[j] = alpha * l_sc[j] + jnp.sum(p, axis=-1, keepdims=True)
            acc_sc[j] = alpha * acc_sc[j] + jnp.dot(p.astype(BF16), v, preferred_element_type=F32)
            m_sc[j] = m_new

    def body(ki, carry):
        block(ki, False)
        return carry

    lax.fori_loop(0, qi, body, 0)
    block(qi, True)
    for j in range(hp):
        o_ref[:, cols[j]] = (acc_sc[j] / l_sc[j]).astype(o_ref.dtype)


def prompt_attention(qn, kb, vb, c, ck_row, n_seq, t, n_heads, m_rows):
    tq = min(ATT_Q, t)
    nq = t // tq
    hp = 2 if n_heads % 2 == 0 else 1
    wd = hp * HEAD_DIM
    kv_spec = pl.BlockSpec((t, wd), lambda b, h, i: (b, h))
    return pl.pallas_call(
        functools.partial(_flash_kernel, HEAD_DIM ** -0.5, tq, hp),
        grid=(n_seq, n_heads // hp, nq),
        in_specs=[
            pl.BlockSpec((tq, wd), lambda b, h, i: (b * nq + i, h)),
            kv_spec, kv_spec,
            pl.BlockSpec((tq, LANES), lambda b, h, i: (b * nq + i, 0)),
            pl.BlockSpec((None, hp, nq, 1, tq), lambda b, h, i: (b, h, 0, 0, 0)),
        ],
        out_specs=pl.BlockSpec((tq, wd), lambda b, h, i: (b * nq + i, h)),
        out_shape=jax.ShapeDtypeStruct((m_rows, n_heads * HEAD_DIM), BF16),
        scratch_shapes=[pltpu.VMEM((hp, tq, 1), F32), pltpu.VMEM((hp, tq, 1), F32),
                        pltpu.VMEM((hp, tq, HEAD_DIM), F32)],
        compiler_params=_cp("parallel", "parallel", "arbitrary"),
    )(qn, kb, vb, c, ck_row)


def _decay_prefix_kernel(n_pages_step, pt_ref, *refs):
    lf_refs = refs[:n_pages_step]
    o_ref = refs[n_pages_step]
    carry_ref = refs[n_pages_step + 1]

    @pl.when(pl.program_id(1) == 0)
    def _init():
        carry_ref[...] = jnp.zeros_like(carry_ref)

    ps = lf_refs[0].shape[0]
    r = lax.broadcasted_iota(I32, (ps, ps), 0)
    c = lax.broadcasted_iota(I32, (ps, ps), 1)
    upper = jnp.where(c > r, 1.0, 0.0).astype(BF16)
    carry = carry_ref[...]
    for idx in range(n_pages_step):
        x = lf_refs[idx][...]
        suffix = _dot_exact_lhs01(upper, x) + carry
        slot = n_pages_step - 1 - idx
        o_ref[slot * ps:(slot + 1) * ps, :] = suffix
        carry = suffix[0:1, :] + x[0:1, :]
    carry_ref[...] = carry


def decay_prefix(cache_lf, page_table):
    depth, _, ps, nh = cache_lf.shape
    nb, n_pages = page_table.shape
    pp = math.gcd(PREP_PAGES, n_pages)
    steps = n_pages // pp

    def lf_spec(idx):
        def imap(lb, j, pt):
            page = n_pages - 1 - (j * pp + idx)
            return (lb // nb, pt[(lb % nb) * n_pages + page], 0, 0)
        return pl.BlockSpec((None, None, ps, nh), imap)

    grid_spec = pltpu.PrefetchScalarGridSpec(
        num_scalar_prefetch=1,
        grid=(depth * nb, steps),
        in_specs=[lf_spec(idx) for idx in range(pp)],
        out_specs=pl.BlockSpec((None, None, pp * ps, nh),
                               lambda lb, j, pt: (lb // nb, lb % nb, steps - 1 - j, 0)),
        scratch_shapes=[pltpu.VMEM((1, nh), F32)],
    )
    return pl.pallas_call(
        functools.partial(_decay_prefix_kernel, pp),
        grid_spec=grid_spec,
        out_shape=jax.ShapeDtypeStruct((depth, nb, n_pages * ps, nh), F32),
        compiler_params=_cp("parallel", "arbitrary"),
    )(page_table.reshape(-1), *([cache_lf] * pp))


def _decode_kernel(scale, n_pages_step, n_heads, pt_ref, q_ref, kn_ref, vn_ref, lfn_ref, bias_ref, *refs):
    k_refs = refs[:n_pages_step]
    v_refs = refs[n_pages_step:2 * n_pages_step]
    o_ref = refs[2 * n_pages_step]
    m_sc, l_sc, acc_sc = refs[2 * n_pages_step + 1:]
    j = pl.program_id(1)

    @pl.when(j == 0)
    def _init():
        m_sc[...] = jnp.full_like(m_sc, NEG_INF)
        l_sc[...] = jnp.zeros_like(l_sc)
        acc_sc[...] = jnp.zeros_like(acc_sc)

    q = q_ref[...]
    q_hi, q_lo = _split2(q)
    width = k_refs[0].shape[0]
    own = (lax.broadcasted_iota(I32, (n_heads, width), 1) % n_heads
           == lax.broadcasted_iota(I32, (n_heads, width), 0))
    lfn = lfn_ref[...]
    qk = lambda a, b: lax.dot_general(a, b, (((1,), (1,)), ((), ())), preferred_element_type=F32)
    pv = functools.partial(jnp.dot, preferred_element_type=F32)
    for idx in range(n_pages_step):
        k_hi, k_lo = _split2(k_refs[idx][...])
        s = (qk(q_hi, k_hi) + qk(q_lo, k_hi) + qk(q_hi, k_lo)) * scale
        s = jnp.where(own, s + bias_ref[idx] + lfn, NEG_INF)
        m_prev = m_sc[...]
        m_new = jnp.maximum(m_prev, jnp.max(s, axis=-1, keepdims=True))
        alpha = jnp.exp(m_prev - m_new)
        p = jnp.exp(s - m_new)
        l_sc[...] = alpha * l_sc[...] + jnp.sum(p, axis=-1, keepdims=True)
        p_hi, p_lo = _split2(p)
        v_hi, v_lo = _split2(v_refs[idx][...])
        acc_sc[...] = alpha * acc_sc[...] + pv(p_hi, v_hi) + pv(p_lo, v_hi) + pv(p_hi, v_lo)
        m_sc[...] = m_new

    @pl.when(j == pl.num_programs(1) - 1)
    def _finish():
        s = jnp.sum(q * kn_ref[...], axis=-1, keepdims=True) * scale
        m_prev = m_sc[...]
        m_new = jnp.maximum(m_prev, s)
        alpha = jnp.exp(m_prev - m_new)
        p = jnp.exp(s - m_new)
        l = alpha * l_sc[...] + p
        acc = alpha * acc_sc[...] + p * vn_ref[...]
        o_ref[...] = (acc / l).astype(o_ref.dtype)


def decode_attention(q, k_new, v_new, lf_new_tiled, bias, cache_k, cache_v, page_table, layer):
    nb, nh, _ = q.shape
    n_pages = page_table.shape[1]
    width = cache_k.shape[2]
    pp = math.gcd(DEC_PAGES, n_pages)
    steps = n_pages // pp

    def page_spec(idx):
        return pl.BlockSpec((None, None, width, HEAD_DIM),
                            lambda b, j, pt: (layer, pt[b * n_pages + j * pp + idx], 0, 0))

    tok = pl.BlockSpec((None, nh, HEAD_DIM), lambda b, j, pt: (b, 0, 0))
    grid_spec = pltpu.PrefetchScalarGridSpec(
        num_scalar_prefetch=1,
        grid=(nb, steps),
        in_specs=[tok, tok, tok,
                  pl.BlockSpec((None, 1, width), lambda b, j, pt: (b, 0, 0)),
                  pl.BlockSpec((None, pp, 1, width), lambda b, j, pt: (b, j, 0, 0))]
                 + [page_spec(idx) for idx in range(pp)] * 2,
        out_specs=pl.BlockSpec((None, nh, HEAD_DIM), lambda b, j, pt: (b, 0, 0)),
        scratch_shapes=[pltpu.VMEM((nh, 1), F32), pltpu.VMEM((nh, 1), F32), pltpu.VMEM((nh, HEAD_DIM), F32)],
    )
    return pl.pallas_call(
        functools.partial(_decode_kernel, HEAD_DIM ** -0.5, pp, nh),
        grid_spec=grid_spec,
        out_shape=jax.ShapeDtypeStruct((nb, nh, HEAD_DIM), F32),
        compiler_params=_cp("parallel", "arbitrary"),
    )(page_table.reshape(-1), q, k_new, v_new, lf_new_tiled, bias, *([cache_k] * pp), *([cache_v] * pp))


def _s5_dot(a, b_ref_slice):
    if b_ref_slice.dtype == F32:
        return _dot_f32(a, b_ref_slice)
    return jnp.dot(a.astype(BF16), b_ref_slice, preferred_element_type=F32)


def _s5_input(u_ref, b_ref, coef_ref):
    w = coef_ref.shape[-1]
    bu = _s5_dot(u_ref[...], b_ref[...])
    bu_re, bu_im = bu[:, :w], bu[:, w:]
    cr, ci = coef_ref[0:1, :], coef_ref[1:2, :]
    return cr * bu_re - ci * bu_im, cr * bu_im + ci * bu_re


def _s5_output(h_re, h_im, u_ref, c_ref, d_ref):
    w = h_re.shape[-1]
    y = _s5_dot(h_re, c_ref[:w, :]) + _s5_dot(h_im, c_ref[w:, :]) + d_ref[...] * u_ref[...]
    return jax.nn.gelu(y)


def _s5_scan_kernel(u_ref, b_ref, c_ref, coef_ref, ab_ref, d_ref, g_ref, gb_ref, hre_ref, him_ref,
                    xre_sc, xim_sc, carry_sc, cin_sc):
    tc = pl.program_id(2)

    @pl.when(tc == 0)
    def _init():
        carry_sc[...] = jnp.zeros_like(carry_sc)

    lt, w = xre_sc.shape
    nseg = 8
    seg = lt // nseg
    p_row = lax.broadcasted_iota(I32, (lt, lt), 0)
    t_col = lax.broadcasted_iota(I32, (lt, lt), 1)
    perm = jnp.where(t_col == (p_row % nseg) * seg + p_row // nseg, 1.0, 0.0).astype(BF16)
    unperm = jnp.where(p_row == (t_col % nseg) * seg + t_col // nseg, 1.0, 0.0).astype(BF16)
    u_perm = jnp.dot(perm, u_ref[...].astype(BF16), preferred_element_type=F32).astype(BF16)
    x_re, x_im = _s5_input(u_perm, b_ref, coef_ref)
    xre_sc[...] = x_re
    xim_sc[...] = x_im
    ar, ai = ab_ref[0:1, :], ab_ref[1:2, :]
    seg_rows = lambda i: pl.ds(pl.multiple_of(i * nseg, nseg), nseg)

    def load_rows(sc, i):
        return sc[seg_rows(i), :]

    def store_rows(sc, i, val):
        sc[seg_rows(i), :] = val

    ar8, ai8 = jnp.broadcast_to(ar, (nseg, w)), jnp.broadcast_to(ai, (nseg, w))

    def step(i, carry):
        hr, hi = carry
        nr = ar8 * hr - ai8 * hi + load_rows(xre_sc, i)
        ni = ar8 * hi + ai8 * hr + load_rows(xim_sc, i)
        store_rows(xre_sc, i, nr)
        store_rows(xim_sc, i, ni)
        return nr, ni

    zero = jnp.zeros((nseg, w), F32)
    loc_r, loc_i = lax.fori_loop(0, seg, step, (zero, zero), unroll=2)

    pr, pi = ar, ai
    for _ in range(seg.bit_length() - 1):
        pr, pi = pr * pr - pi * pi, 2.0 * pr * pi
    hr, hi = carry_sc[0:1, :], carry_sc[1:2, :]
    for j in range(nseg):
        cin_sc[0, j:j + 1, :] = hr
        cin_sc[1, j:j + 1, :] = hi
        hr, hi = (pr * hr - pi * hi + loc_r[j:j + 1, :], pr * hi + pi * hr + loc_i[j:j + 1, :])
    carry_sc[0:1, :] = hr
    carry_sc[1:2, :] = hi

    cin_r, cin_i = cin_sc[0], cin_sc[1]

    def fix(i, carry):
        qr, qi = carry
        store_rows(xre_sc, i, load_rows(xre_sc, i) + (qr * cin_r - qi * cin_i))
        store_rows(xim_sc, i, load_rows(xim_sc, i) + (qr * cin_i + qi * cin_r))
        return ar8 * qr - ai8 * qi, ar8 * qi + ai8 * qr

    lax.fori_loop(0, seg, fix, (ar8, ai8), unroll=2)
    y_perm = _s5_dot(xre_sc[...], c_ref[:w, :]) + _s5_dot(xim_sc[...], c_ref[w:, :])
    g = jax.nn.gelu(_dot_exact_lhs01(unperm, y_perm) + d_ref[...] * u_ref[...])
    g_ref[...] = g
    gb_ref[...] = g.astype(BF16)

    @pl.when(tc == pl.num_programs(2) - 1)
    def _final_state():
        hre_ref[...] = hr
        him_ref[...] = hi


def s5_scan(rest, bcat, ccat, coef, ab, dskip, n_seq, t, m_rows, d_ssm):
    ngb, cw, w2 = bcat.shape
    w = w2 // 2
    lt = min(S5_CHUNK, t)
    nt = t // lt
    assert lt % 8 == 0 and (lt // 8) & (lt // 8 - 1) == 0
    par = lambda shape: pl.BlockSpec((None,) + shape, lambda b, g, j: (g, 0, 0))
    row = pl.BlockSpec((lt, cw), lambda b, g, j: (b * nt + j, g))
    st = pl.BlockSpec((None, 1, w), lambda b, g, j: (b, 0, g))
    return pl.pallas_call(
        _s5_scan_kernel,
        grid=(n_seq, ngb, nt),
        in_specs=[row, par((cw, w2)), par((w2, cw)), par((2, w)), par((2, w)), par((1, cw))],
        out_specs=[row, row, st, st],
        out_shape=[jax.ShapeDtypeStruct((m_rows, d_ssm), F32), jax.ShapeDtypeStruct((m_rows, d_ssm), BF16),
                   jax.ShapeDtypeStruct((n_seq, 1, ngb * w), F32), jax.ShapeDtypeStruct((n_seq, 1, ngb * w), F32)],
        scratch_shapes=[pltpu.VMEM((lt, w), F32), pltpu.VMEM((lt, w), F32),
                        pltpu.VMEM((2, w), F32), pltpu.VMEM((2, 8, w), F32)],
        compiler_params=_cp("parallel", "parallel", "arbitrary"),
    )(rest, bcat, ccat, coef, ab, dskip)


def _s5_step_kernel(n_real, u_ref, b_ref, c_ref, coef_ref, ab_ref, d_ref, h0re_ref, h0im_ref,
                    g_ref, gb_ref, hre_ref, him_ref):
    x_re, x_im = _s5_input(u_ref, b_ref, coef_ref)
    ar, ai = ab_ref[0:1, :], ab_ref[1:2, :]
    h0r, h0i = h0re_ref[...], h0im_ref[...]
    h_re = x_re + ar * h0r - ai * h0i
    h_im = x_im + ar * h0i + ai * h0r
    g = _s5_output(h_re, h_im, u_ref, c_ref, d_ref)
    g_ref[...] = g
    gb_ref[...] = g.astype(BF16)
    hre_ref[...] = h_re[:n_real, :]
    him_ref[...] = h_im[:n_real, :]


def s5_step(rest, row0, rows, n_real, h0_re, h0_im, bcat, ccat, coef, ab, dskip, d_ssm):
    ngb, cw, w2 = bcat.shape
    w = w2 // 2
    rb = row0 // rows
    par = lambda shape: pl.BlockSpec((None,) + shape, lambda g: (g, 0, 0))
    st = pl.BlockSpec((rows, w), lambda g: (0, g))
    sto = pl.BlockSpec((n_real, w), lambda g: (0, g))
    tile = pl.BlockSpec((rows, cw), lambda g: (0, g))
    return pl.pallas_call(
        functools.partial(_s5_step_kernel, n_real),
        grid=(ngb,),
        in_specs=[pl.BlockSpec((rows, cw), lambda g: (rb, g)), par((cw, w2)), par((w2, cw)), par((2, w)),
                  par((2, w)), par((1, cw)), st, st],
        out_specs=[tile, tile, sto, sto],
        out_shape=[jax.ShapeDtypeStruct((rows, d_ssm), F32), jax.ShapeDtypeStruct((rows, d_ssm), BF16),
                   jax.ShapeDtypeStruct((n_real, ngb * w), F32), jax.ShapeDtypeStruct((n_real, ngb * w), F32)],
        compiler_params=_cp("parallel"),
    )(rest, bcat, ccat, coef, ab, dskip, h0_re, h0_im)


def s5_parameters(a_re, a_im, log_dt, b_re, b_im, c_re, c_im, d_skip):
    g, n, ch = b_re.shape
    gb = math.gcd(S5_GROUPS, g)
    ngb = g // gb
    dt = jnp.exp(log_dt)[:, None]
    lr, li = a_re * dt, a_im * dt
    mag = jnp.exp(lr)
    ab_re, ab_im = mag * jnp.cos(li), mag * jnp.sin(li)
    nr, ni = ab_re - 1.0, ab_im
    den = a_re * a_re + a_im * a_im
    coef_re = (nr * a_re + ni * a_im) / den
    coef_im = (ni * a_re - nr * a_im) / den
    blk = lambda x: x.reshape(ngb, 1, gb * n)
    coef = jnp.concatenate([blk(coef_re), blk(coef_im)], axis=1)
    ab = jnp.concatenate([blk(ab_re), blk(ab_im)], axis=1)
    eye = jnp.eye(gb, dtype=F32)

    def bdiag_in(b):
        bb = b.reshape(ngb, gb, n, ch)
        return jnp.einsum('kgnc,gh->kgchn', bb, eye).reshape(ngb, gb * ch, gb * n)

    def bdiag_out(c):
        cc = c.reshape(ngb, gb, ch, n)
        return jnp.einsum('kgcn,gh->kgnhc', cc, eye).reshape(ngb, gb * n, gb * ch)

    bcat = jnp.concatenate([bdiag_in(b_re), bdiag_in(b_im)], axis=2)
    ccat = jnp.concatenate([bdiag_out(c_re), -bdiag_out(c_im)], axis=1)
    return bcat, ccat, coef, ab, d_skip.reshape(ngb, 1, gb * ch)


def _rows_from_chunks(ref, rows):
    ch = ref.shape[0] // rows
    return jnp.concatenate([ref[pl.ds(c, rows, stride=ch), :] for c in range(ch)], axis=1)


def _rows_to_chunks(ref, val):
    rows = val.shape[0]
    ch = ref.shape[0] // rows
    for c in range(ch):
        ref[pl.ds(c, rows, stride=ch), :] = val[:, c * LANES:(c + 1) * LANES]


def _router_kernel(n_groups, per_group, x_ref, g_ref, w_ref, b_ref, hn_ref, eid_ref, wts_ref):
    x = x_ref[...]
    hn = x * lax.rsqrt(jnp.mean(x * x, axis=-1, keepdims=True) + RMS_EPS) * g_ref[...]
    _rows_to_chunks(hn_ref, hn)
    logits = jnp.dot(hn, w_ref[...], preferred_element_type=F32, precision=lax.Precision.HIGHEST) + b_ref[...]
    lane = lax.broadcasted_iota(I32, logits.shape, 1).astype(F32)
    far = float(LANES)
    red_max = lambda v: jnp.max(v, axis=-1, keepdims=True)
    red_min = lambda v: jnp.min(v, axis=-1, keepdims=True)
    gmask = lane < n_groups
    gl = jnp.where(gmask, logits, -jnp.inf)
    gmax = red_max(gl)
    g_top = 1.0 / jnp.sum(jnp.where(gmask, jnp.exp(gl - gmax), 0.0), axis=-1, keepdims=True)
    g_idx = red_min(jnp.where(gl == gmax, lane, far))
    lo = n_groups + g_idx * per_group
    el = jnp.where((lane >= lo) & (lane < lo + per_group), logits, -jnp.inf)
    t1 = red_max(el)
    i1 = red_min(jnp.where(el == t1, lane, far))
    el2 = jnp.where(lane == i1, -jnp.inf, el)
    t2 = red_max(el2)
    i2 = red_min(jnp.where(el2 == t2, lane, far))
    e21 = jnp.exp(t2 - t1)
    w1 = 1.0 / (1.0 + e21)
    w2 = e21 * w1
    eid_ref[...] = jnp.where(lane == 0.0, i1 - n_groups, jnp.where(lane == 1.0, i2 - n_groups, 0.0)).astype(I32)
    wts_ref[...] = jnp.where(lane == 0.0, w1 * g_top, jnp.where(lane == 1.0, w2 * g_top, 0.0))


def moe_route(x, g, w_router, b_router, n_groups, per_group, tr):
    m, d = x.shape
    ch = d // LANES
    rows = lambda wd: pl.BlockSpec((tr, wd), lambda i: (i, 0))
    return pl.pallas_call(
        functools.partial(_router_kernel, n_groups, per_group),
        grid=(m // tr,),
        in_specs=[rows(d), pl.BlockSpec((1, d), lambda i: (0, 0)), pl.BlockSpec((d, LANES), lambda i: (0, 0)),
                  pl.BlockSpec((1, LANES), lambda i: (0, 0))],
        out_specs=[pl.BlockSpec((tr * ch, LANES), lambda i: (i, 0)), rows(LANES), rows(LANES)],
        out_shape=[jax.ShapeDtypeStruct((m * ch, LANES), F32), jax.ShapeDtypeStruct((m, LANES), I32),
                   jax.ShapeDtypeStruct((m, LANES), F32)],
        compiler_params=_cp("parallel"),
    )(x, g.reshape(1, d), w_router, b_router.reshape(1, LANES))


def moe_schedule(eid, n_experts, tile_rows):
    m = eid.shape[0]
    e = eid[:, :TOP_K_IN_GROUP].reshape(-1)
    n_assign = e.shape[0]
    n_tiles = -(-n_assign // tile_rows) + n_experts
    onehot = (e[:, None] == jnp.arange(n_experts, dtype=I32)[None, :]).astype(I32)
    counts = jnp.sum(onehot, axis=0)
    rank = jnp.take_along_axis(jnp.cumsum(onehot, axis=0), e[:, None], axis=1)[:, 0] - 1
    padded = -(-counts // tile_rows) * tile_rows
    ends = jnp.cumsum(padded)
    dest = (ends - padded)[e] + rank
    src_token = jnp.zeros((n_tiles * tile_rows,), I32).at[dest].set(jnp.arange(n_assign, dtype=I32) // TOP_K_IN_GROUP)
    tile_row0 = jnp.arange(n_tiles, dtype=I32) * tile_rows
    tile_valid = (tile_row0 < ends[-1]).astype(I32)
    tile_expert = jnp.sum((ends[None, :] <= jnp.minimum(tile_row0, ends[-1] - 1)[:, None]).astype(I32), axis=1)
    tile_first = jnp.concatenate([jnp.ones((1,), I32), (tile_expert[1:] != tile_expert[:-1]).astype(I32)])
    slot_rows = dest.reshape(m, TOP_K_IN_GROUP).T.reshape(-1)
    tile_read = jnp.arange(n_tiles, dtype=I32) * tile_valid
    return src_token, slot_rows, (tile_expert, tile_first, tile_valid, tile_read)


def _gather_kernel(rows_step, ch, idx_ref, live_ref, src_ref, out_ref, sem):
    step = pl.program_id(0)
    base = step * rows_step

    def copy(r):
        src0 = pl.multiple_of(idx_ref[base + r] * ch, ch)
        dst0 = pl.multiple_of(r * ch, ch)
        return pltpu.make_async_copy(src_ref.at[pl.ds(src0, ch)], out_ref.at[pl.ds(dst0, ch)], sem)

    def issue(r, carry):
        copy(r).start()
        return carry

    def drain(r, carry):
        copy(r).wait()
        return carry

    @pl.when(live_ref[step] == 1)
    def _copy_rows():
        lax.fori_loop(0, rows_step, issue, 0)
        lax.fori_loop(0, rows_step, drain, 0)

    @pl.when(live_ref[step] == 0)
    def _unused_rows():
        out_ref[...] = jnp.zeros_like(out_ref)


def gather_rows(src, idx, step_live, rows_step, ch):
    n = idx.shape[0]
    grid_spec = pltpu.PrefetchScalarGridSpec(
        num_scalar_prefetch=2,
        grid=(n // rows_step,),
        in_specs=[pl.BlockSpec(memory_space=pl.ANY)],
        out_specs=pl.BlockSpec((rows_step * ch, LANES), lambda s, idx_ref, live_ref: (s, 0)),
        scratch_shapes=[pltpu.SemaphoreType.DMA(())],
    )
    return pl.pallas_call(
        functools.partial(_gather_kernel, rows_step, ch),
        grid_spec=grid_spec,
        out_shape=jax.ShapeDtypeStruct((n * ch, LANES), src.dtype),
        compiler_params=_cp("arbitrary"),
    )(idx, step_live, src)


def _moe_up_kernel(te_ref, tf_ref, tv_ref, tr_ref, x_ref, wg_ref, wu_ref, o_ref, wgb_ref, wub_ref):
    t = pl.program_id(1)

    @pl.when(tf_ref[t] == 1)
    def _cast_weights():
        wgb_ref[...] = wg_ref[...].astype(BF16)
        wub_ref[...] = wu_ref[...].astype(BF16)

    @pl.when(tv_ref[t] == 1)
    def _compute():
        x = _rows_from_chunks(x_ref, o_ref.shape[0]).astype(BF16)
        a = jnp.dot(x, wgb_ref[...], preferred_element_type=F32)
        b = jnp.dot(x, wub_ref[...], preferred_element_type=F32)
        o_ref[...] = (a * jax.nn.sigmoid(a) * b).astype(o_ref.dtype)

    @pl.when(tv_ref[t] == 0)
    def _unused_tile():
        o_ref[...] = jnp.zeros_like(o_ref)


def _moe_up_f32_kernel(te_ref, tf_ref, tv_ref, tr_ref, x_ref, wg_ref, wu_ref, o_ref):
    t = pl.program_id(1)

    @pl.when(tv_ref[t] == 1)
    def _compute():
        x = _rows_from_chunks(x_ref, o_ref.shape[0])
        a = _dot_f32(x, wg_ref[...])
        o_ref[...] = a * jax.nn.sigmoid(a) * _dot_f32(x, wu_ref[...])

    @pl.when(tv_ref[t] == 0)
    def _unused_tile():
        o_ref[...] = jnp.zeros_like(o_ref)


def moe_up(xs, w_gate, w_up, layer, tiles, tile_rows, f32_dots=False):
    d, f = w_gate.shape[-2:]
    ch = d // LANES
    r = xs.shape[0] // ch
    fc = _col_tile(f, 256)
    w_spec = pl.BlockSpec((None, None, d, fc), lambda c, t, te, tf, tv, tr: (layer, te[t], 0, c))
    grid_spec = pltpu.PrefetchScalarGridSpec(
        num_scalar_prefetch=4,
        grid=(f // fc, r // tile_rows),
        in_specs=[pl.BlockSpec((tile_rows * ch, LANES), lambda c, t, te, tf, tv, tr: (tr[t], 0)), w_spec, w_spec],
        out_specs=pl.BlockSpec((tile_rows, fc), lambda c, t, te, tf, tv, tr: (t, c)),
        scratch_shapes=[] if f32_dots else [pltpu.VMEM((d, fc), BF16), pltpu.VMEM((d, fc), BF16)],
    )
    return pl.pallas_call(
        _moe_up_f32_kernel if f32_dots else _moe_up_kernel, grid_spec=grid_spec,
        out_shape=jax.ShapeDtypeStruct((r, f), F32 if f32_dots else BF16),
        compiler_params=_cp("arbitrary", "arbitrary"),
    )(*tiles, xs, w_gate, w_up)


def _moe_down_kernel(te_ref, tf_ref, tv_ref, tr_ref, h_ref, wd_ref, o_ref, wdb_ref):
    t = pl.program_id(0)

    @pl.when(tf_ref[t] == 1)
    def _cast_weights():
        wdb_ref[...] = wd_ref[...].astype(BF16)

    @pl.when(tv_ref[t] == 1)
    def _compute():
        _rows_to_chunks(o_ref, jnp.dot(h_ref[...], wdb_ref[...], preferred_element_type=F32))

    @pl.when(tv_ref[t] == 0)
    def _unused_tile():
        o_ref[...] = jnp.zeros_like(o_ref)


def _moe_down_f32_kernel(te_ref, tf_ref, tv_ref, tr_ref, h_ref, wd_ref, o_ref):
    t = pl.program_id(0)

    @pl.when(tv_ref[t] == 1)
    def _compute():
        _rows_to_chunks(o_ref, _dot_f32(h_ref[...], wd_ref[...]))

    @pl.when(tv_ref[t] == 0)
    def _unused_tile():
        o_ref[...] = jnp.zeros_like(o_ref)


def moe_down(hid, w_down, layer, tiles, tile_rows, f32_dots=False):
    r, f = hid.shape
    d = w_down.shape[-1]
    ch = d // LANES
    grid_spec = pltpu.PrefetchScalarGridSpec(
        num_scalar_prefetch=4,
        grid=(r // tile_rows,),
        in_specs=[pl.BlockSpec((tile_rows, f), lambda t, te, tf, tv, tr: (t, 0)),
                  pl.BlockSpec((None, None, f, d), lambda t, te, tf, tv, tr: (layer, te[t], 0, 0))],
        out_specs=pl.BlockSpec((tile_rows * ch, LANES), lambda t, te, tf, tv, tr: (t, 0)),
        scratch_shapes=[] if f32_dots else [pltpu.VMEM((f, d), BF16)],
    )
    return pl.pallas_call(
        _moe_down_f32_kernel if f32_dots else _moe_down_kernel, grid_spec=grid_spec,
        out_shape=jax.ShapeDtypeStruct((r * ch, LANES), F32),
        compiler_params=_cp("arbitrary"),
    )(*tiles, hid, w_down)


def _combine_kernel(x_ref, y0_ref, y1_ref, w_ref, g_ref, x2_ref, hn_ref):
    w = w_ref[...]
    rows = x_ref.shape[0]
    x2 = (x_ref[...] + w[:, 0:1] * _rows_from_chunks(y0_ref, rows) + w[:, 1:2] * _rows_from_chunks(y1_ref, rows))
    x2_ref[...] = x2
    hn = x2 * lax.rsqrt(jnp.mean(x2 * x2, axis=-1, keepdims=True) + RMS_EPS) * g_ref[...]
    hn_ref[...] = hn.astype(hn_ref.dtype)


def moe_combine(x, y_slots, wts, g, tr, act):
    m, d = x.shape
    nb = m // tr
    ch = d // LANES
    rows = pl.BlockSpec((tr, d), lambda i: (i, 0))
    return pl.pallas_call(
        _combine_kernel,
        grid=(nb,),
        in_specs=[rows, pl.BlockSpec((tr * ch, LANES), lambda i: (i, 0)),
                  pl.BlockSpec((tr * ch, LANES), lambda i: (i + nb, 0)),
                  pl.BlockSpec((tr, LANES), lambda i: (i, 0)), pl.BlockSpec((1, d), lambda i: (0, 0))],
        out_specs=[rows, rows],
        out_shape=[jax.ShapeDtypeStruct((m, d), F32), jax.ShapeDtypeStruct((m, d), act)],
        compiler_params=_cp("parallel"),
    )(x, y_slots, y_slots, wts, g.reshape(1, d))


def kernel(x_prompt, x_sample, cache_k, cache_v, cache_lf, state_ssm_re, state_ssm_im, page_table, p_prompt, p_sample, g_mix, w_in, g_q, g_k, b_f, ssm_a_re, ssm_a_im, ssm_log_dt, ssm_b_re, ssm_b_im, ssm_c_re, ssm_c_im, ssm_d, w_glu, w_br_ssm, w_br_att, w_out, g_ffn, router_group_w, router_group_b, router_expert_w, router_expert_b, w_gate, w_up, w_down, g_ple, w_ple_gate, w_ple_proj):
    nb_p, t, d = x_prompt.shape
    nb_s = x_sample.shape[0]
    assert x_sample.shape[1] == 1
    depth = w_in.shape[0]
    nh = b_f.shape[1]
    da = nh * HEAD_DIM
    d_ssm = ssm_d.shape[1]
    n_grp, n_state, _ = ssm_b_re.shape[1:]
    n_groups = router_group_w.shape[2]
    n_experts = router_expert_w.shape[2]
    per_group = n_experts // n_groups
    ps = cache_k.shape[2]
    n_pages = page_table.shape[1]
    ple = p_prompt.shape[-1]
    assert n_groups + n_experts <= LANES and nh <= LANES

    bt = nb_p * t
    tm = bt // max(1, round(bt / ROW_TILE_TARGET))
    assert bt % tm == 0 and tm % BF16_SUBLANES == 0 and nb_s % 8 == 0
    tn = _col_tile(da, 512)
    tn_m = _col_tile(d, 256)
    cfg_p = (bt, tm, _row_divisor(tm, ELEM_ROWS_MAX), MOE_ROWS, False)
    cfg_s = (nb_s, nb_s, nb_s, 8, True)

    xp = x_prompt.reshape(bt, d)
    xs = x_sample.reshape(nb_s, d)
    pp_all = p_prompt.reshape(depth, bt, ple)
    ps_all = p_sample.reshape(depth, nb_s, ple)

    cache_k2 = cache_k.reshape(depth, -1, ps * nh, HEAD_DIM)
    cache_v2 = cache_v.reshape(depth, -1, ps * nh, HEAD_DIM)
    past_bias = decay_prefix(cache_lf, page_table).reshape(depth, nb_s, n_pages, 1, ps * nh)
    h0_re = state_ssm_re.reshape(depth, nb_s, -1)
    h0_im = state_ssm_im.reshape(depth, nb_s, -1)
    w_router = jnp.concatenate([router_group_w, router_expert_w,
                                jnp.zeros((depth, d, LANES - n_groups - n_experts), F32)], axis=2)
    b_router = jnp.concatenate([router_group_b, router_expert_b,
                                jnp.zeros((depth, LANES - n_groups - n_experts), F32)], axis=1)
    rest_col0 = 3 * da + nh
    n_rest = w_in.shape[2] - rest_col0
    assert n_rest == d_ssm + 2 * d and da % tn == 0 and d_ssm % tn == 0 and d % tn == 0

    ident = lambda accs, ex: accs
    glu = lambda accs, ex: [ex[0] * jax.nn.sigmoid(accs[0])]
    gated = lambda accs, ex: [jax.nn.sigmoid(ex[0]) * accs[0] + jax.nn.sigmoid(ex[1]) * accs[1]]
    resid = lambda accs, ex: [ex[0] + accs[0]]
    ple_gate = lambda accs, ex: [ex[0] + jax.nn.sigmoid(accs[0]) * accs[1]]

    def project(x, i, cfg):
        m, tmc, tr, _, hi = cfg
        act = F32 if hi else BF16
        h = rmsnorm_rows(x, g_mix[i], tr, act)
        (qkv,) = fused_matmul([(h, 0, w_in, i, 0)], [], ident, [F32], m, 3 * da, tmc, tn, hi)
        (f_logit,) = fused_matmul([(h, 0, w_in, i, 3 * da // LANES)], [], ident, [F32], m, LANES, tmc, LANES, hi)
        (rest,) = fused_matmul([(h, 0, w_in, i, 3 * da // tn)], [], ident, [F32], m, n_rest, tmc, tn, hi,
                               shift=nh)
        b_f_pad = jnp.concatenate([b_f[i], jnp.zeros((LANES - nh,), F32)]).reshape(1, LANES)
        return (qkv, rest) + tuple(qk_prepare(qkv, f_logit, g_q[i], g_k[i], b_f_pad, tr, nh, act))

    def mix_and_ffn(x, p_all, att, g_f, g_act, rest, i, cfg):
        m, tmc, tr, moe_rows, hi = cfg
        act = F32 if hi else BF16
        (s_out,) = fused_matmul([(g_act, 0, w_glu, i, 0)], [(g_f, 0)], glu, [act], m, d_ssm, tmc, tn, hi)
        (merged,) = fused_matmul([(s_out, 0, w_br_ssm, i, 0), (att, 0, w_br_att, i, 0)],
                                 [(rest, d_ssm // tn_m), (rest, (d_ssm + d) // tn_m)], gated, [act],
                                 m, d, tmc, tn_m, hi)
        (x,) = fused_matmul([(merged, 0, w_out, i, 0)], [(x, 0)], resid, [F32], m, d, tmc, tn, hi)
        hn, eid, wts = moe_route(x, g_ffn[i], w_router[i], b_router[i], n_groups, per_group, tr)
        src_token, slot_rows, tiles = moe_schedule(eid, n_experts, moe_rows)
        x_sorted = gather_rows(hn, src_token, tiles[2], moe_rows, d // LANES)
        hid = moe_up(x_sorted, w_gate, w_up, i, tiles, moe_rows, hi)
        y_sorted = moe_down(hid, w_down, i, tiles, moe_rows, hi)
        slot_step = math.gcd(DMA_ROWS, slot_rows.shape[0])
        y_slots = gather_rows(y_sorted, slot_rows, jnp.ones((slot_rows.shape[0] // slot_step,), I32), slot_step,
                              d // LANES)
        x, hn2 = moe_combine(x, y_slots, wts, g_ple[i], tr, act)
        (x,) = fused_matmul([(hn2, 0, w_ple_gate, i, 0), (p_all, i, w_ple_proj, i, 0)], [(x, 0)], ple_gate,
                            [F32], m, d, tmc, tn_m, hi)
        return x

    outs = [[] for _ in range(10)]
    for i in range(depth):
        qkv_p, rest_p, qn, kn_p, kb, vb, lf_p = project(xp, i, cfg_p)
        qkv_s, rest_s, qn_s, kn_s, _, vb_s, lf_s = project(xs, i, cfg_s)

        c = cumsum_time(lf_p, nb_p, t, min(256, t))
        tq = min(ATT_Q, t)
        ck_row = c[:, :nh].reshape(nb_p, t, nh).transpose(0, 2, 1).reshape(nb_p, nh, t // tq, 1, tq)
        att_p = prompt_attention(qn, kb, vb, c, ck_row, nb_p, t, nh, bt)
        tok3 = lambda a: a.reshape(nb_s, nh, HEAD_DIM)
        lf_new = jnp.tile(lf_s[:, :nh], (1, ps)).reshape(nb_s, 1, ps * nh)
        att_s = decode_attention(tok3(qn_s), tok3(kn_s), tok3(vb_s), lf_new, past_bias[i], cache_k2, cache_v2,
                                 page_table, i).reshape(nb_s, da)

        bcat, ccat, coef, ab, dskip = s5_parameters(ssm_a_re[i], ssm_a_im[i], ssm_log_dt[i], ssm_b_re[i],
                                                    ssm_b_im[i], ssm_c_re[i], ssm_c_im[i], ssm_d[i])
        gp_f, gp_b, hp_re, hp_im = s5_scan(rest_p, bcat.astype(BF16), ccat.astype(BF16), coef, ab, dskip,
                                           nb_p, t, bt, d_ssm)
        gs_f, _, hs_re, hs_im = s5_step(rest_s, 0, nb_s, nb_s, h0_re[i], h0_im[i], bcat, ccat, coef, ab, dskip,
                                        d_ssm)

        xp = mix_and_ffn(xp, pp_all, att_p, gp_f, gp_b, rest_p, i, cfg_p)
        xs = mix_and_ffn(xs, ps_all, att_s, gs_f, gs_f, rest_s, i, cfg_s)

        per_layer = (kn_p.reshape(nb_p, t, nh, HEAD_DIM), qkv_p[:, 2 * da:].reshape(nb_p, t, nh, HEAD_DIM),
                     lf_p[:, :nh].reshape(nb_p, t, nh),
                     hp_re.reshape(nb_p, n_grp, n_state), hp_im.reshape(nb_p, n_grp, n_state),
                     kn_s.reshape(nb_s, 1, nh, HEAD_DIM), qkv_s[:, 2 * da:].reshape(nb_s, 1, nh, HEAD_DIM),
                     lf_s[:, :nh].reshape(nb_s, 1, nh),
                     hs_re.reshape(nb_s, n_grp, n_state), hs_im.reshape(nb_s, n_grp, n_state))
        for lst, val in zip(outs, per_layer):
            lst.append(val)

    return (xp.reshape(nb_p, t, d), xs.reshape(nb_s, 1, d)) + tuple(jnp.stack(lst) for lst in outs)
```

```python
import functools
import math

import jax
import jax.numpy as jnp
from jax import lax
from jax.experimental import pallas as pl
from jax.experimental.pallas import tpu as pltpu

F32 = jnp.float32
BF16 = jnp.bfloat16
I32 = jnp.int32

RMS_EPS = 1e-6
NEG_INF = -1e30
HEAD_DIM = 128
LANES = 128
BF16_SUBLANES = 16
TOP_K_IN_GROUP = 2
V7X_VMEM_LIMIT_BYTES = 58 * 1024 * 1024

ROW_TILE_TARGET = 1024
ELEM_ROWS_MAX = 256
MOE_ROWS = 256
DMA_ROWS = 256
ATT_Q = 512
DEC_PAGES = 4
PREP_PAGES = 8
S5_CHUNK = 256
S5_GROUPS = 16


def _cp(*sem):
    return pltpu.CompilerParams(dimension_semantics=sem, vmem_limit_bytes=V7X_VMEM_LIMIT_BYTES)


def _round_up(x, m):
    return -(-x // m) * m


def _row_divisor(tm, max_rows):
    best = BF16_SUBLANES
    for r in range(BF16_SUBLANES, max_rows + 1, BF16_SUBLANES):
        if tm % r == 0:
            best = r
    return best


def _col_tile(n, target):
    t = min(n, target)
    while n % t:
        t //= 2
    return t


def _split3(x):
    hi = x.astype(BF16)
    r1 = x - hi.astype(F32)
    mid = r1.astype(BF16)
    lo = (r1 - mid.astype(F32)).astype(BF16)
    return hi, mid, lo


def _split2(x):
    hi = x.astype(BF16)
    return hi, (x - hi.astype(F32)).astype(BF16)


def _dot_exact_lhs01(tri, x):
    hi, mid, lo = _split3(x)
    d = functools.partial(jnp.dot, preferred_element_type=F32)
    return d(tri, hi) + d(tri, mid) + d(tri, lo)


def _rmsnorm_kernel(x_ref, g_ref, o_ref):
    x = x_ref[...]
    y = x * lax.rsqrt(jnp.mean(x * x, axis=-1, keepdims=True) + RMS_EPS)
    o_ref[...] = (y * g_ref[...]).astype(o_ref.dtype)


def rmsnorm_rows(x, g, tr, out_dtype):
    m, d = x.shape
    return pl.pallas_call(
        _rmsnorm_kernel,
        grid=(m // tr,),
        in_specs=[pl.BlockSpec((tr, d), lambda i: (i, 0)), pl.BlockSpec((1, d), lambda i: (0, 0))],
        out_specs=pl.BlockSpec((tr, d), lambda i: (i, 0)),
        out_shape=jax.ShapeDtypeStruct((m, d), out_dtype),
        compiler_params=_cp("parallel"),
    )(x, g.reshape(1, d))


def _dot_f32(a, b):
    return jnp.dot(a, b, preferred_element_type=F32, precision=lax.Precision.HIGHEST)


def _shifted_rows(w_ref, wn_ref, shift, r0, r1):
    if not shift:
        return w_ref[r0:r1, :]
    return jnp.concatenate([w_ref[r0:r1, shift:], wn_ref[r0:r1, :shift]], axis=1)


def _fused_matmul_f32_kernel(n_dots, n_extra, epilogue, shift, *refs):
    a_refs = refs[:n_dots]
    w_refs = refs[n_dots:2 * n_dots]
    refs = refs[2 * n_dots:]
    wn_ref = refs[0] if shift else None
    refs = refs[1:] if shift else refs
    e_refs = refs[:n_extra]
    o_refs = refs[n_extra:]
    accs = [_dot_f32(a_ref[...], _shifted_rows(w_ref, wn_ref, shift, 0, w_ref.shape[0]))
            for a_ref, w_ref in zip(a_refs, w_refs)]
    outs = epilogue(accs, [e_ref[...] for e_ref in e_refs])
    for o_ref, o in zip(o_refs, outs):
        o_ref[...] = o.astype(o_ref.dtype)


def _fused_matmul_kernel(n_dots, n_extra, n_out, epilogue, cast_rows, shift, *refs):
    a_refs = refs[:n_dots]
    w_refs = refs[n_dots:2 * n_dots]
    refs = refs[2 * n_dots:]
    wn_ref = refs[0] if shift else None
    refs = refs[1:] if shift else refs
    e_refs = refs[:n_extra]
    o_refs = refs[n_extra:n_extra + n_out]
    wb_refs = refs[n_extra + n_out:]

    @pl.when(pl.program_id(1) == 0)
    def _cast_weights():
        for w_ref, wb_ref in zip(w_refs, wb_refs):
            k = w_ref.shape[0]
            for r0 in range(0, k, cast_rows):
                r1 = min(k, r0 + cast_rows)
                wb_ref[r0:r1, :] = _shifted_rows(w_ref, wn_ref, shift, r0, r1).astype(BF16)

    accs = [jnp.dot(a_ref[...].astype(BF16), wb_ref[...], preferred_element_type=F32)
            for a_ref, wb_ref in zip(a_refs, wb_refs)]
    outs = epilogue(accs, [e_ref[...] for e_ref in e_refs])
    for o_ref, o in zip(o_refs, outs):
        o_ref[...] = o.astype(o_ref.dtype)


def fused_matmul(dots, extras, epilogue, out_dtypes, m, n, tm, tn, f32_dots=False, shift=0):
    assert not shift or (len(dots) == 1 and dots[0][2].ndim == 3 and shift < LANES)
    in_specs, args, scratch = [], [], []
    for a, a_lead, _, _, _ in dots:
        k = a.shape[-1]
        if a.ndim == 3:
            in_specs.append(pl.BlockSpec((None, tm, k), lambda j, i, l=a_lead: (l, i, 0)))
        else:
            in_specs.append(pl.BlockSpec((tm, k), lambda j, i: (i, 0)))
        args.append(a)
    for _, _, w, w_lead, w_cb in dots:
        k = w.shape[-2]
        if w.ndim == 3:
            in_specs.append(pl.BlockSpec((None, k, tn), lambda j, i, l=w_lead, c=w_cb: (l, 0, j + c)))
        else:
            in_specs.append(pl.BlockSpec((k, tn), lambda j, i, c=w_cb: (0, j + c)))
        args.append(w)
        if not f32_dots:
            scratch.append(pltpu.VMEM((k, tn), BF16))
    if shift:
        _, _, w, w_lead, w_cb = dots[0]
        per = tn // LANES
        in_specs.append(pl.BlockSpec((None, w.shape[-2], LANES),
                                     lambda j, i, l=w_lead, c=w_cb: (l, 0, (j + c + 1) * per)))
        args.append(w)
    for e, e_cb in extras:
        in_specs.append(pl.BlockSpec((tm, tn), lambda j, i, c=e_cb: (i, j + c)))
        args.append(e)
    n_out = len(out_dtypes)
    if f32_dots:
        kern = functools.partial(_fused_matmul_f32_kernel, len(dots), len(extras), epilogue, shift)
    else:
        kern = functools.partial(_fused_matmul_kernel, len(dots), len(extras), n_out, epilogue, 512, shift)
    outs = pl.pallas_call(
        kern,
        grid=(n // tn, m // tm),
        in_specs=in_specs,
        out_specs=[pl.BlockSpec((tm, tn), lambda j, i: (i, j)) for _ in out_dtypes],
        out_shape=[jax.ShapeDtypeStruct((m, n), dt) for dt in out_dtypes],
        scratch_shapes=scratch,
        compiler_params=_cp("arbitrary", "arbitrary"),
    )(*args)
    return outs


def _qkprep_kernel(n_heads, q_ref, k_ref, v_ref, f_ref, gq_ref, gk_ref, bf_ref,
                   qn_ref, kn_ref, kb_ref, vb_ref, lf_ref):
    gq = gq_ref[...]
    gk = gk_ref[...]
    for h in range(n_heads):
        sl = slice(h * HEAD_DIM, (h + 1) * HEAD_DIM)
        q = q_ref[:, sl]
        qn = q * lax.rsqrt(jnp.mean(q * q, axis=-1, keepdims=True) + RMS_EPS) * gq
        qn_ref[:, sl] = qn.astype(qn_ref.dtype)
        k = k_ref[:, sl]
        kn = k * lax.rsqrt(jnp.mean(k * k, axis=-1, keepdims=True) + RMS_EPS) * gk
        kn_ref[:, sl] = kn
        kb_ref[:, sl] = kn.astype(kb_ref.dtype)
    vb_ref[...] = v_ref[...].astype(vb_ref.dtype)
    z = f_ref[...] + bf_ref[...]
    lf_ref[...] = jnp.minimum(z, 0.0) - jnp.log1p(jnp.exp(-jnp.abs(z)))


def qk_prepare(qkv, f, g_q, g_k, b_f_pad, tr, n_heads, act):
    m = qkv.shape[0]
    da = n_heads * HEAD_DIM
    row = lambda c: pl.BlockSpec((tr, da), lambda i, c=c: (i, c))
    vec = pl.BlockSpec((1, LANES), lambda i: (0, 0))
    big = lambda dt: jax.ShapeDtypeStruct((m, da), dt)
    return pl.pallas_call(
        functools.partial(_qkprep_kernel, n_heads),
        grid=(m // tr,),
        in_specs=[row(0), row(1), row(2), pl.BlockSpec((tr, LANES), lambda i: (i, 0)), vec, vec, vec],
        out_specs=[row(0), row(0), row(0), row(0), pl.BlockSpec((tr, LANES), lambda i: (i, 0))],
        out_shape=[big(act), big(F32), big(act), big(act), jax.ShapeDtypeStruct((m, LANES), F32)],
        compiler_params=_cp("parallel"),
    )(qkv, qkv, qkv, f, g_q.reshape(1, HEAD_DIM), g_k.reshape(1, HEAD_DIM), b_f_pad)


def _cumsum_kernel(x_ref, o_ref, carry_ref):
    @pl.when(pl.program_id(1) == 0)
    def _init():
        carry_ref[...] = jnp.zeros_like(carry_ref)

    lc = x_ref.shape[0]
    r = lax.broadcasted_iota(I32, (lc, lc), 0)
    c = lax.broadcasted_iota(I32, (lc, lc), 1)
    tri = jnp.where(c <= r, 1.0, 0.0).astype(BF16)
    out = _dot_exact_lhs01(tri, x_ref[...]) + carry_ref[...]
    o_ref[...] = out
    carry_ref[...] = out[lc - 1:lc, :]


def cumsum_time(x, n_seq, t, lc):
    m = x.shape[0]
    nb = t // lc
    return pl.pallas_call(
        _cumsum_kernel,
        grid=(n_seq, nb),
        in_specs=[pl.BlockSpec((lc, LANES), lambda b, j: (b * nb + j, 0))],
        out_specs=pl.BlockSpec((lc, LANES), lambda b, j: (b * nb + j, 0)),
        out_shape=jax.ShapeDtypeStruct((n_seq * t, LANES), F32),
        scratch_shapes=[pltpu.VMEM((1, LANES), F32)],
        compiler_params=_cp("parallel", "arbitrary"),
    )(x)


def _flash_kernel(scale, tq, hp, q_ref, k_ref, v_ref, c_ref, ck_ref, o_ref, m_sc, l_sc, acc_sc):
    qi = pl.program_id(2)
    lane = lax.broadcasted_iota(I32, (tq, LANES), 1)
    c_blk = c_ref[...]
    cols = [slice(j * HEAD_DIM, (j + 1) * HEAD_DIM) for j in range(hp)]
    cq = [jnp.sum(jnp.where(lane == pl.program_id(1) * hp + j, c_blk, 0.0), axis=-1, keepdims=True)
          for j in range(hp)]
    m_sc[...] = jnp.full_like(m_sc, NEG_INF)
    l_sc[...] = jnp.zeros_like(l_sc)
    acc_sc[...] = jnp.zeros_like(acc_sc)

    def block(ki, diagonal):
        off = pl.multiple_of(ki * tq, tq)
        for j in range(hp):
            k = k_ref[pl.ds(off, tq), cols[j]]
            v = v_ref[pl.ds(off, tq), cols[j]]
            s = lax.dot_general(q_ref[:, cols[j]], k, (((1,), (1,)), ((), ())),
                                preferred_element_type=F32) * scale
            s = s + cq[j] - ck_ref[j, ki]
            if diagonal:
                causal = (lax.broadcasted_iota(I32, (tq, tq), 1) <= lax.broadcasted_iota(I32, (tq, tq), 0))
                s = jnp.where(causal, s, NEG_INF)
            m_prev = m_sc[j]
            m_new = jnp.maximum(m_prev, jnp.max(s, axis=-1, keepdims=True))
            alpha = jnp.exp(m_prev - m_new)
            p = jnp.exp(s - m_new)
            l_sc[j] = alpha * l_sc[j] + jnp.sum(p, axis=-1, keepdims=True)
            acc_sc[j] = alpha * acc_sc[j] + jnp.dot(p.astype(BF16), v, preferred_element_type=F32)
            m_sc[j] = m_new

    def body(ki, carry):
        block(ki, False)
        return carry

    lax.fori_loop(0, qi, body, 0)
    block(qi, True)
    for j in range(hp):
        o_ref[:, cols[j]] = (acc_sc[j] / l_sc[j]).astype(o_ref.dtype)


def prompt_attention(qn, kb, vb, c, ck_row, n_seq, t, n_heads, m_rows):
    tq = min(ATT_Q, t)
    nq = t // tq
    hp = 2 if n_heads % 2 == 0 else 1
    wd = hp * HEAD_DIM
    kv_spec = pl.BlockSpec((t, wd), lambda b, h, i: (b, h))
    return pl.pallas_call(
        functools.partial(_flash_kernel, HEAD_DIM ** -0.5, tq, hp),
        grid=(n_seq, n_heads // hp, nq),
        in_specs=[
            pl.BlockSpec((tq, wd), lambda b, h, i: (b * nq + i, h)),
            kv_spec, kv_spec,
            pl.BlockSpec((tq, LANES), lambda b, h, i: (b * nq + i, 0)),
            pl.BlockSpec((None, hp, nq, 1, tq), lambda b, h, i: (b, h, 0, 0, 0)),
        ],
        out_specs=pl.BlockSpec((tq, wd), lambda b, h, i: (b * nq + i, h)),
        out_shape=jax.ShapeDtypeStruct((m_rows, n_heads * HEAD_DIM), BF16),
        scratch_shapes=[pltpu.VMEM((hp, tq, 1), F32), pltpu.VMEM((hp, tq, 1), F32),
                        pltpu.VMEM((hp, tq, HEAD_DIM), F32)],
        compiler_params=_cp("parallel", "parallel", "arbitrary"),
    )(qn, kb, vb, c, ck_row)


def _decay_prefix_kernel(n_pages_step, pt_ref, *refs):
    lf_refs = refs[:n_pages_step]
    o_ref = refs[n_pages_step]
    carry_ref = refs[n_pages_step + 1]

    @pl.when(pl.program_id(1) == 0)
    def _init():
        carry_ref[...] = jnp.zeros_like(carry_ref)

    ps = lf_refs[0].shape[0]
    r = lax.broadcasted_iota(I32, (ps, ps), 0)
    c = lax.broadcasted_iota(I32, (ps, ps), 1)
    upper = jnp.where(c > r, 1.0, 0.0).astype(BF16)
    carry = carry_ref[...]
    for idx in range(n_pages_step):
        x = lf_refs[idx][...]
        suffix = _dot_exact_lhs01(upper, x) + carry
        slot = n_pages_step - 1 - idx
        o_ref[slot * ps:(slot + 1) * ps, :] = suffix
        carry = suffix[0:1, :] + x[0:1, :]
    carry_ref[...] = carry


def decay_prefix(cache_lf, page_table):
    depth, _, ps, nh = cache_lf.shape
    nb, n_pages = page_table.shape
    pp = math.gcd(PREP_PAGES, n_pages)
    steps = n_pages // pp

    def lf_spec(idx):
        def imap(lb, j, pt):
            page = n_pages - 1 - (j * pp + idx)
            return (lb // nb, pt[(lb % nb) * n_pages + page], 0, 0)
        return pl.BlockSpec((None, None, ps, nh), imap)

    grid_spec = pltpu.PrefetchScalarGridSpec(
        num_scalar_prefetch=1,
        grid=(depth * nb, steps),
        in_specs=[lf_spec(idx) for idx in range(pp)],
        out_specs=pl.BlockSpec((None, None, pp * ps, nh),
                               lambda lb, j, pt: (lb // nb, lb % nb, steps - 1 - j, 0)),
        scratch_shapes=[pltpu.VMEM((1, nh), F32)],
    )
    return pl.pallas_call(
        functools.partial(_decay_prefix_kernel, pp),
        grid_spec=grid_spec,
        out_shape=jax.ShapeDtypeStruct((depth, nb, n_pages * ps, nh), F32),
        compiler_params=_cp("parallel", "arbitrary"),
    )(page_table.reshape(-1), *([cache_lf] * pp))


def _decode_kernel(scale, n_pages_step, n_heads, pt_ref, q_ref, kn_ref, vn_ref, lfn_ref, bias_ref, *refs):
    k_refs = refs[:n_pages_step]
    v_refs = refs[n_pages_step:2 * n_pages_step]
    o_ref = refs[2 * n_pages_step]
    m_sc, l_sc, acc_sc = refs[2 * n_pages_step + 1:]
    j = pl.program_id(1)

    @pl.when(j == 0)
    def _init():
        m_sc[...] = jnp.full_like(m_sc, NEG_INF)
        l_sc[...] = jnp.zeros_like(l_sc)
        acc_sc[...] = jnp.zeros_like(acc_sc)

    q = q_ref[...]
    q_hi, q_lo = _split2(q)
    width = k_refs[0].shape[0]
    own = (lax.broadcasted_iota(I32, (n_heads, width), 1) % n_heads
           == lax.broadcasted_iota(I32, (n_heads, width), 0))
    lfn = lfn_ref[...]
    qk = lambda a, b: lax.dot_general(a, b, (((1,), (1,)), ((), ())), preferred_element_type=F32)
    pv = functools.partial(jnp.dot, preferred_element_type=F32)
    for idx in range(n_pages_step):
        k_hi, k_lo = _split2(k_refs[idx][...])
        s = (qk(q_hi, k_hi) + qk(q_lo, k_hi) + qk(q_hi, k_lo)) * scale
        s = jnp.where(own, s + bias_ref[idx] + lfn, NEG_INF)
        m_prev = m_sc[...]
        m_new = jnp.maximum(m_prev, jnp.max(s, axis=-1, keepdims=True))
        alpha = jnp.exp(m_prev - m_new)
        p = jnp.exp(s - m_new)
        l_sc[...] = alpha * l_sc[...] + jnp.sum(p, axis=-1, keepdims=True)
        p_hi, p_lo = _split2(p)
        v_hi, v_lo = _split2(v_refs[idx][...])
        acc_sc[...] = alpha * acc_sc[...] + pv(p_hi, v_hi) + pv(p_lo, v_hi) + pv(p_hi, v_lo)
        m_sc[...] = m_new

    @pl.when(j == pl.num_programs(1) - 1)
    def _finish():
        s = jnp.sum(q * kn_ref[...], axis=-1, keepdims=True) * scale
        m_prev = m_sc[...]
        m_new = jnp.maximum(m_prev, s)
        alpha = jnp.exp(m_prev - m_new)
        p = jnp.exp(s - m_new)
        l = alpha * l_sc[...] + p
        acc = alpha * acc_sc[...] + p * vn_ref[...]
        o_ref[...] = (acc / l).astype(o_ref.dtype)


def decode_attention(q, k_new, v_new, lf_new_tiled, bias, cache_k, cache_v, page_table, layer):
    nb, nh, _ = q.shape
    n_pages = page_table.shape[1]
    width = cache_k.shape[2]
    pp = math.gcd(DEC_PAGES, n_pages)
    steps = n_pages // pp

    def page_spec(idx):
        return pl.BlockSpec((None, None, width, HEAD_DIM),
                            lambda b, j, pt: (layer, pt[b * n_pages + j * pp + idx], 0, 0))

    tok = pl.BlockSpec((None, nh, HEAD_DIM), lambda b, j, pt: (b, 0, 0))
    grid_spec = pltpu.PrefetchScalarGridSpec(
        num_scalar_prefetch=1,
        grid=(nb, steps),
        in_specs=[tok, tok, tok,
                  pl.BlockSpec((None, 1, width), lambda b, j, pt: (b, 0, 0)),
                  pl.BlockSpec((None, pp, 1, width), lambda b, j, pt: (b, j, 0, 0))]
                 + [page_spec(idx) for idx in range(pp)] * 2,
        out_specs=pl.BlockSpec((None, nh, HEAD_DIM), lambda b, j, pt: (b, 0, 0)),
        scratch_shapes=[pltpu.VMEM((nh, 1), F32), pltpu.VMEM((nh, 1), F32), pltpu.VMEM((nh, HEAD_DIM), F32)],
    )
    return pl.pallas_call(
        functools.partial(_decode_kernel, HEAD_DIM ** -0.5, pp, nh),
        grid_spec=grid_spec,
        out_shape=jax.ShapeDtypeStruct((nb, nh, HEAD_DIM), F32),
        compiler_params=_cp("parallel", "arbitrary"),
    )(page_table.reshape(-1), q, k_new, v_new, lf_new_tiled, bias, *([cache_k] * pp), *([cache_v] * pp))


def _s5_dot(a, b_ref_slice):
    if b_ref_slice.dtype == F32:
        return _dot_f32(a, b_ref_slice)
    return jnp.dot(a.astype(BF16), b_ref_slice, preferred_element_type=F32)


def _s5_input(u_ref, b_ref, coef_ref):
    w = coef_ref.shape[-1]
    bu = _s5_dot(u_ref[...], b_ref[...])
    bu_re, bu_im = bu[:, :w], bu[:, w:]
    cr, ci = coef_ref[0:1, :], coef_ref[1:2, :]
    return cr * bu_re - ci * bu_im, cr * bu_im + ci * bu_re


def _s5_output(h_re, h_im, u_ref, c_ref, d_ref):
    w = h_re.shape[-1]
    y = _s5_dot(h_re, c_ref[:w, :]) + _s5_dot(h_im, c_ref[w:, :]) + d_ref[...] * u_ref[...]
    return jax.nn.gelu(y)


def _s5_scan_kernel(u_ref, b_ref, c_ref, coef_ref, ab_ref, d_ref, g_ref, gb_ref, hre_ref, him_ref,
                    xre_sc, xim_sc, carry_sc, cin_sc):
    tc = pl.program_id(2)

    @pl.when(tc == 0)
    def _init():
        carry_sc[...] = jnp.zeros_like(carry_sc)

    lt, w = xre_sc.shape
    nseg = 8
    seg = lt // nseg
    p_row = lax.broadcasted_iota(I32, (lt, lt), 0)
    t_col = lax.broadcasted_iota(I32, (lt, lt), 1)
    perm = jnp.where(t_col == (p_row % nseg) * seg + p_row // nseg, 1.0, 0.0).astype(BF16)
    unperm = jnp.where(p_row == (t_col % nseg) * seg + t_col // nseg, 1.0, 0.0).astype(BF16)
    u_perm = jnp.dot(perm, u_ref[...].astype(BF16), preferred_element_type=F32).astype(BF16)
    x_re, x_im = _s5_input(u_perm, b_ref, coef_ref)
    xre_sc[...] = x_re
    xim_sc[...] = x_im
    ar, ai = ab_ref[0:1, :], ab_ref[1:2, :]
    seg_rows = lambda i: pl.ds(pl.multiple_of(i * nseg, nseg), nseg)

    def load_rows(sc, i):
        return sc[seg_rows(i), :]

    def store_rows(sc, i, val):
        sc[seg_rows(i), :] = val

    ar8, ai8 = jnp.broadcast_to(ar, (nseg, w)), jnp.broadcast_to(ai, (nseg, w))

    def step(i, carry):
        hr, hi = carry
        nr = ar8 * hr - ai8 * hi + load_rows(xre_sc, i)
        ni = ar8 * hi + ai8 * hr + load_rows(xim_sc, i)
        store_rows(xre_sc, i, nr)
        store_rows(xim_sc, i, ni)
        return nr, ni

    zero = jnp.zeros((nseg, w), F32)
    loc_r, loc_i = lax.fori_loop(0, seg, step, (zero, zero), unroll=2)

    pr, pi = ar, ai
    for _ in range(seg.bit_length() - 1):
        pr, pi = pr * pr - pi * pi, 2.0 * pr * pi
    hr, hi = carry_sc[0:1, :], carry_sc[1:2, :]
    for j in range(nseg):
        cin_sc[0, j:j + 1, :] = hr
        cin_sc[1, j:j + 1, :] = hi
        hr, hi = (pr * hr - pi * hi + loc_r[j:j + 1, :], pr * hi + pi * hr + loc_i[j:j + 1, :])
    carry_sc[0:1, :] = hr
    carry_sc[1:2, :] = hi

    cin_r, cin_i = cin_sc[0], cin_sc[1]

    def fix(i, carry):
        qr, qi = carry
        store_rows(xre_sc, i, load_rows(xre_sc, i) + (qr * cin_r - qi * cin_i))
        store_rows(xim_sc, i, load_rows(xim_sc, i) + (qr * cin_i + qi * cin_r))
        return ar8 * qr - ai8 * qi, ar8 * qi + ai8 * qr

    lax.fori_loop(0, seg, fix, (ar8, ai8), unroll=2)
    y_perm = _s5_dot(xre_sc[...], c_ref[:w, :]) + _s5_dot(xim_sc[...], c_ref[w:, :])
    g = jax.nn.gelu(_dot_exact_lhs01(unperm, y_perm) + d_ref[...] * u_ref[...])
    g_ref[...] = g
    gb_ref[...] = g.astype(BF16)

    @pl.when(tc == pl.num_programs(2) - 1)
    def _final_state():
        hre_ref[...] = hr
        him_ref[...] = hi


def s5_scan(rest, bcat, ccat, coef, ab, dskip, n_seq, t, m_rows, d_ssm):
    ngb, cw, w2 = bcat.shape
    w = w2 // 2
    lt = min(S5_CHUNK, t)
    nt = t // lt
    assert lt % 8 == 0 and (lt // 8) & (lt // 8 - 1) == 0
    par = lambda shape: pl.BlockSpec((None,) + shape, lambda b, g, j: (g, 0, 0))
    row = pl.BlockSpec((lt, cw), lambda b, g, j: (b * nt + j, g))
    st = pl.BlockSpec((None, 1, w), lambda b, g, j: (b, 0, g))
    return pl.pallas_call(
        _s5_scan_kernel,
        grid=(n_seq, ngb, nt),
        in_specs=[row, par((cw, w2)), par((w2, cw)), par((2, w)), par((2, w)), par((1, cw))],
        out_specs=[row, row, st, st],
        out_shape=[jax.ShapeDtypeStruct((m_rows, d_ssm), F32), jax.ShapeDtypeStruct((m_rows, d_ssm), BF16),
                   jax.ShapeDtypeStruct((n_seq, 1, ngb * w), F32), jax.ShapeDtypeStruct((n_seq, 1, ngb * w), F32)],
        scratch_shapes=[pltpu.VMEM((lt, w), F32), pltpu.VMEM((lt, w), F32),
                        pltpu.VMEM((2, w), F32), pltpu.VMEM((2, 8, w), F32)],
        compiler_params=_cp("parallel", "parallel", "arbitrary"),
    )(rest, bcat, ccat, coef, ab, dskip)


def _s5_step_kernel(n_real, u_ref, b_ref, c_ref, coef_ref, ab_ref, d_ref, h0re_ref, h0im_ref,
                    g_ref, gb_ref, hre_ref, him_ref):
    x_re, x_im = _s5_input(u_ref, b_ref, coef_ref)
    ar, ai = ab_ref[0:1, :], ab_ref[1:2, :]
    h0r, h0i = h0re_ref[...], h0im_ref[...]
    h_re = x_re + ar * h0r - ai * h0i
    h_im = x_im + ar * h0i + ai * h0r
    g = _s5_output(h_re, h_im, u_ref, c_ref, d_ref)
    g_ref[...] = g
    gb_ref[...] = g.astype(BF16)
    hre_ref[...] = h_re[:n_real, :]
    him_ref[...] = h_im[:n_real, :]


def s5_step(rest, row0, rows, n_real, h0_re, h0_im, bcat, ccat, coef, ab, dskip, d_ssm):
    ngb, cw, w2 = bcat.shape
    w = w2 // 2
    rb = row0 // rows
    par = lambda shape: pl.BlockSpec((None,) + shape, lambda g: (g, 0, 0))
    st = pl.BlockSpec((rows, w), lambda g: (0, g))
    sto = pl.BlockSpec((n_real, w), lambda g: (0, g))
    tile = pl.BlockSpec((rows, cw), lambda g: (0, g))
    return pl.pallas_call(
        functools.partial(_s5_step_kernel, n_real),
        grid=(ngb,),
        in_specs=[pl.BlockSpec((rows, cw), lambda g: (rb, g)), par((cw, w2)), par((w2, cw)), par((2, w)),
                  par((2, w)), par((1, cw)), st, st],
        out_specs=[tile, tile, sto, sto],
        out_shape=[jax.ShapeDtypeStruct((rows, d_ssm), F32), jax.ShapeDtypeStruct((rows, d_ssm), BF16),
                   jax.ShapeDtypeStruct((n_real, ngb * w), F32), jax.ShapeDtypeStruct((n_real, ngb * w), F32)],
        compiler_params=_cp("parallel"),
    )(rest, bcat, ccat, coef, ab, dskip, h0_re, h0_im)


def s5_parameters(a_re, a_im, log_dt, b_re, b_im, c_re, c_im, d_skip):
    g, n, ch = b_re.shape
    gb = math.gcd(S5_GROUPS, g)
    ngb = g // gb
    dt = jnp.exp(log_dt)[:, None]
    lr, li = a_re * dt, a_im * dt
    mag = jnp.exp(lr)
    ab_re, ab_im = mag * jnp.cos(li), mag * jnp.sin(li)
    nr, ni = ab_re - 1.0, ab_im
    den = a_re * a_re + a_im * a_im
    coef_re = (nr * a_re + ni * a_im) / den
    coef_im = (ni * a_re - nr * a_im) / den
    blk = lambda x: x.reshape(ngb, 1, gb * n)
    coef = jnp.concatenate([blk(coef_re), blk(coef_im)], axis=1)
    ab = jnp.concatenate([blk(ab_re), blk(ab_im)], axis=1)
    eye = jnp.eye(gb, dtype=F32)

    def bdiag_in(b):
        bb = b.reshape(ngb, gb, n, ch)
        return jnp.einsum('kgnc,gh->kgchn', bb, eye).reshape(ngb, gb * ch, gb * n)

    def bdiag_out(c):
        cc = c.reshape(ngb, gb, ch, n)
        return jnp.einsum('kgcn,gh->kgnhc', cc, eye).reshape(ngb, gb * n, gb * ch)

    bcat = jnp.concatenate([bdiag_in(b_re), bdiag_in(b_im)], axis=2)
    ccat = jnp.concatenate([bdiag_out(c_re), -bdiag_out(c_im)], axis=1)
    return bcat, ccat, coef, ab, d_skip.reshape(ngb, 1, gb * ch)


def _rows_from_chunks(ref, rows):
    ch = ref.shape[0] // rows
    return jnp.concatenate([ref[pl.ds(c, rows, stride=ch), :] for c in range(ch)], axis=1)


def _rows_to_chunks(ref, val):
    rows = val.shape[0]
    ch = ref.shape[0] // rows
    for c in range(ch):
        ref[pl.ds(c, rows, stride=ch), :] = val[:, c * LANES:(c + 1) * LANES]


def _router_kernel(n_groups, per_group, x_ref, g_ref, w_ref, b_ref, hn_ref, eid_ref, wts_ref):
    x = x_ref[...]
    hn = x * lax.rsqrt(jnp.mean(x * x, axis=-1, keepdims=True) + RMS_EPS) * g_ref[...]
    _rows_to_chunks(hn_ref, hn)
    logits = jnp.dot(hn, w_ref[...], preferred_element_type=F32, precision=lax.Precision.HIGHEST) + b_ref[...]
    lane = lax.broadcasted_iota(I32, logits.shape, 1).astype(F32)
    far = float(LANES)
    red_max = lambda v: jnp.max(v, axis=-1, keepdims=True)
    red_min = lambda v: jnp.min(v, axis=-1, keepdims=True)
    gmask = lane < n_groups
    gl = jnp.where(gmask, logits, -jnp.inf)
    gmax = red_max(gl)
    g_top = 1.0 / jnp.sum(jnp.where(gmask, jnp.exp(gl - gmax), 0.0), axis=-1, keepdims=True)
    g_idx = red_min(jnp.where(gl == gmax, lane, far))
    lo = n_groups + g_idx * per_group
    el = jnp.where((lane >= lo) & (lane < lo + per_group), logits, -jnp.inf)
    t1 = red_max(el)
    i1 = red_min(jnp.where(el == t1, lane, far))
    el2 = jnp.where(lane == i1, -jnp.inf, el)
    t2 = red_max(el2)
    i2 = red_min(jnp.where(el2 == t2, lane, far))
    e21 = jnp.exp(t2 - t1)
    w1 = 1.0 / (1.0 + e21)
    w2 = e21 * w1
    eid_ref[...] = jnp.where(lane == 0.0, i1 - n_groups, jnp.where(lane == 1.0, i2 - n_groups, 0.0)).astype(I32)
    wts_ref[...] = jnp.where(lane == 0.0, w1 * g_top, jnp.where(lane == 1.0, w2 * g_top, 0.0))


def moe_route(x, g, w_router, b_router, n_groups, per_group, tr):
    m, d = x.shape
    ch = d // LANES
    rows = lambda wd: pl.BlockSpec((tr, wd), lambda i: (i, 0))
    return pl.pallas_call(
        functools.partial(_router_kernel, n_groups, per_group),
        grid=(m // tr,),
        in_specs=[rows(d), pl.BlockSpec((1, d), lambda i: (0, 0)), pl.BlockSpec((d, LANES), lambda i: (0, 0)),
                  pl.BlockSpec((1, LANES), lambda i: (0, 0))],
        out_specs=[pl.BlockSpec((tr * ch, LANES), lambda i: (i, 0)), rows(LANES), rows(LANES)],
        out_shape=[jax.ShapeDtypeStruct((m * ch, LANES), F32), jax.ShapeDtypeStruct((m, LANES), I32),
                   jax.ShapeDtypeStruct((m, LANES), F32)],
        compiler_params=_cp("parallel"),
    )(x, g.reshape(1, d), w_router, b_router.reshape(1, LANES))


def moe_schedule(eid, n_experts, tile_rows):
    m = eid.shape[0]
    e = eid[:, :TOP_K_IN_GROUP].reshape(-1)
    n_assign = e.shape[0]
    n_tiles = -(-n_assign // tile_rows) + n_experts
    onehot = (e[:, None] == jnp.arange(n_experts, dtype=I32)[None, :]).astype(I32)
    counts = jnp.sum(onehot, axis=0)
    rank = jnp.take_along_axis(jnp.cumsum(onehot, axis=0), e[:, None], axis=1)[:, 0] - 1
    padded = -(-counts // tile_rows) * tile_rows
    ends = jnp.cumsum(padded)
    dest = (ends - padded)[e] + rank
    src_token = jnp.zeros((n_tiles * tile_rows,), I32).at[dest].set(jnp.arange(n_assign, dtype=I32) // TOP_K_IN_GROUP)
    tile_row0 = jnp.arange(n_tiles, dtype=I32) * tile_rows
    tile_valid = (tile_row0 < ends[-1]).astype(I32)
    tile_expert = jnp.sum((ends[None, :] <= jnp.minimum(tile_row0, ends[-1] - 1)[:, None]).astype(I32), axis=1)
    tile_first = jnp.concatenate([jnp.ones((1,), I32), (tile_expert[1:] != tile_expert[:-1]).astype(I32)])
    slot_rows = dest.reshape(m, TOP_K_IN_GROUP).T.reshape(-1)
    tile_read = jnp.arange(n_tiles, dtype=I32) * tile_valid
    return src_token, slot_rows, (tile_expert, tile_first, tile_valid, tile_read)


def _gather_kernel(rows_step, ch, idx_ref, live_ref, src_ref, out_ref, sem):
    step = pl.program_id(0)
    base = step * rows_step

    def copy(r):
        src0 = pl.multiple_of(idx_ref[base + r] * ch, ch)
        dst0 = pl.multiple_of(r * ch, ch)
        return pltpu.make_async_copy(src_ref.at[pl.ds(src0, ch)], out_ref.at[pl.ds(dst0, ch)], sem)

    def issue(r, carry):
        copy(2 * r).start(priority=0)
        copy(2 * r + 1).start(priority=1)
        return carry

    def drain(r, carry):
        copy(r).wait()
        return carry

    @pl.when(live_ref[step] == 1)
    def _copy_rows():
        lax.fori_loop(0, rows_step // 2, issue, 0)
        lax.fori_loop(0, rows_step, drain, 0)

    @pl.when(live_ref[step] == 0)
    def _unused_rows():
        out_ref[...] = jnp.zeros_like(out_ref)


def gather_rows(src, idx, step_live, rows_step, ch):
    n = idx.shape[0]
    grid_spec = pltpu.PrefetchScalarGridSpec(
        num_scalar_prefetch=2,
        grid=(n // rows_step,),
        in_specs=[pl.BlockSpec(memory_space=pl.ANY)],
        out_specs=pl.BlockSpec((rows_step * ch, LANES), lambda s, idx_ref, live_ref: (s, 0)),
        scratch_shapes=[pltpu.SemaphoreType.DMA(())],
    )
    return pl.pallas_call(
        functools.partial(_gather_kernel, rows_step, ch),
        grid_spec=grid_spec,
        out_shape=jax.ShapeDtypeStruct((n * ch, LANES), src.dtype),
        compiler_params=_cp("arbitrary"),
    )(idx, step_live, src)


def _moe_up_kernel(te_ref, tf_ref, tv_ref, tr_ref, x_ref, wg_ref, wu_ref, o_ref, wgb_ref, wub_ref):
    t = pl.program_id(1)

    @pl.when(tf_ref[t] == 1)
    def _cast_weights():
        wgb_ref[...] = wg_ref[...].astype(BF16)
        wub_ref[...] = wu_ref[...].astype(BF16)

    @pl.when(tv_ref[t] == 1)
    def _compute():
        x = _rows_from_chunks(x_ref, o_ref.shape[0]).astype(BF16)
        a = jnp.dot(x, wgb_ref[...], preferred_element_type=F32)
        b = jnp.dot(x, wub_ref[...], preferred_element_type=F32)
        o_ref[...] = (a * jax.nn.sigmoid(a) * b).astype(o_ref.dtype)

    @pl.when(tv_ref[t] == 0)
    def _unused_tile():
        o_ref[...] = jnp.zeros_like(o_ref)


def _moe_up_f32_kernel(te_ref, tf_ref, tv_ref, tr_ref, x_ref, wg_ref, wu_ref, o_ref):
    t = pl.program_id(1)

    @pl.when(tv_ref[t] == 1)
    def _compute():
        x = _rows_from_chunks(x_ref, o_ref.shape[0])
        a = _dot_f32(x, wg_ref[...])
        o_ref[...] = a * jax.nn.sigmoid(a) * _dot_f32(x, wu_ref[...])

    @pl.when(tv_ref[t] == 0)
    def _unused_tile():
        o_ref[...] = jnp.zeros_like(o_ref)


def moe_up(xs, w_gate, w_up, layer, tiles, tile_rows, f32_dots=False):
    d, f = w_gate.shape[-2:]
    ch = d // LANES
    r = xs.shape[0] // ch
    fc = _col_tile(f, 256)
    w_spec = pl.BlockSpec((None, None, d, fc), lambda c, t, te, tf, tv, tr: (layer, te[t], 0, c))
    grid_spec = pltpu.PrefetchScalarGridSpec(
        num_scalar_prefetch=4,
        grid=(f // fc, r // tile_rows),
        in_specs=[pl.BlockSpec((tile_rows * ch, LANES), lambda c, t, te, tf, tv, tr: (tr[t], 0)), w_spec, w_spec],
        out_specs=pl.BlockSpec((tile_rows, fc), lambda c, t, te, tf, tv, tr: (t, c)),
        scratch_shapes=[] if f32_dots else [pltpu.VMEM((d, fc), BF16), pltpu.VMEM((d, fc), BF16)],
    )
    return pl.pallas_call(
        _moe_up_f32_kernel if f32_dots else _moe_up_kernel, grid_spec=grid_spec,
        out_shape=jax.ShapeDtypeStruct((r, f), F32 if f32_dots else BF16),
        compiler_params=_cp("arbitrary", "arbitrary"),
    )(*tiles, xs, w_gate, w_up)


def _moe_down_kernel(te_ref, tf_ref, tv_ref, tr_ref, h_ref, wd_ref, o_ref, wdb_ref):
    t = pl.program_id(0)

    @pl.when(tf_ref[t] == 1)
    def _cast_weights():
        wdb_ref[...] = wd_ref[...].astype(BF16)

    @pl.when(tv_ref[t] == 1)
    def _compute():
        _rows_to_chunks(o_ref, jnp.dot(h_ref[...], wdb_ref[...], preferred_element_type=F32))

    @pl.when(tv_ref[t] == 0)
    def _unused_tile():
        o_ref[...] = jnp.zeros_like(o_ref)


def _moe_down_f32_kernel(te_ref, tf_ref, tv_ref, tr_ref, h_ref, wd_ref, o_ref):
    t = pl.program_id(0)

    @pl.when(tv_ref[t] == 1)
    def _compute():
        _rows_to_chunks(o_ref, _dot_f32(h_ref[...], wd_ref[...]))

    @pl.when(tv_ref[t] == 0)
    def _unused_tile():
        o_ref[...] = jnp.zeros_like(o_ref)


def moe_down(hid, w_down, layer, tiles, tile_rows, f32_dots=False):
    r, f = hid.shape
    d = w_down.shape[-1]
    ch = d // LANES
    grid_spec = pltpu.PrefetchScalarGridSpec(
        num_scalar_prefetch=4,
        grid=(r // tile_rows,),
        in_specs=[pl.BlockSpec((tile_rows, f), lambda t, te, tf, tv, tr: (t, 0)),
                  pl.BlockSpec((None, None, f, d), lambda t, te, tf, tv, tr: (layer, te[t], 0, 0))],
        out_specs=pl.BlockSpec((tile_rows * ch, LANES), lambda t, te, tf, tv, tr: (t, 0)),
        scratch_shapes=[] if f32_dots else [pltpu.VMEM((f, d), BF16)],
    )
    return pl.pallas_call(
        _moe_down_f32_kernel if f32_dots else _moe_down_kernel, grid_spec=grid_spec,
        out_shape=jax.ShapeDtypeStruct((r * ch, LANES), F32),
        compiler_params=_cp("arbitrary"),
    )(*tiles, hid, w_down)


def _combine_kernel(x_ref, y0_ref, y1_ref, w_ref, g_ref, x2_ref, hn_ref):
    w = w_ref[...]
    rows = x_ref.shape[0]
    x2 = (x_ref[...] + w[:, 0:1] * _rows_from_chunks(y0_ref, rows) + w[:, 1:2] * _rows_from_chunks(y1_ref, rows))
    x2_ref[...] = x2
    hn = x2 * lax.rsqrt(jnp.mean(x2 * x2, axis=-1, keepdims=True) + RMS_EPS) * g_ref[...]
    hn_ref[...] = hn.astype(hn_ref.dtype)


def moe_combine(x, y_slots, wts, g, tr, act):
    m, d = x.shape
    nb = m // tr
    ch = d // LANES
    rows = pl.BlockSpec((tr, d), lambda i: (i, 0))
    return pl.pallas_call(
        _combine_kernel,
        grid=(nb,),
        in_specs=[rows, pl.BlockSpec((tr * ch, LANES), lambda i: (i, 0)),
                  pl.BlockSpec((tr * ch, LANES), lambda i: (i + nb, 0)),
                  pl.BlockSpec((tr, LANES), lambda i: (i, 0)), pl.BlockSpec((1, d), lambda i: (0, 0))],
        out_specs=[rows, rows],
        out_shape=[jax.ShapeDtypeStruct((m, d), F32), jax.ShapeDtypeStruct((m, d), act)],
        compiler_params=_cp("parallel"),
    )(x, y_slots, y_slots, wts, g.reshape(1, d))


def kernel(x_prompt, x_sample, cache_k, cache_v, cache_lf, state_ssm_re, state_ssm_im, page_table, p_prompt, p_sample, g_mix, w_in, g_q, g_k, b_f, ssm_a_re, ssm_a_im, ssm_log_dt, ssm_b_re, ssm_b_im, ssm_c_re, ssm_c_im, ssm_d, w_glu, w_br_ssm, w_br_att, w_out, g_ffn, router_group_w, router_group_b, router_expert_w, router_expert_b, w_gate, w_up, w_down, g_ple, w_ple_gate, w_ple_proj):
    nb_p, t, d = x_prompt.shape
    nb_s = x_sample.shape[0]
    assert x_sample.shape[1] == 1
    depth = w_in.shape[0]
    nh = b_f.shape[1]
    da = nh * HEAD_DIM
    d_ssm = ssm_d.shape[1]
    n_grp, n_state, _ = ssm_b_re.shape[1:]
    n_groups = router_group_w.shape[2]
    n_experts = router_expert_w.shape[2]
    per_group = n_experts // n_groups
    ps = cache_k.shape[2]
    n_pages = page_table.shape[1]
    ple = p_prompt.shape[-1]
    assert n_groups + n_experts <= LANES and nh <= LANES

    bt = nb_p * t
    tm = bt // max(1, round(bt / ROW_TILE_TARGET))
    assert bt % tm == 0 and tm % BF16_SUBLANES == 0 and nb_s % 8 == 0
    tn = _col_tile(da, 512)
    tn_m = _col_tile(d, 256)
    cfg_p = (bt, tm, _row_divisor(tm, ELEM_ROWS_MAX), MOE_ROWS, False)
    cfg_s = (nb_s, nb_s, nb_s, 8, True)

    xp = x_prompt.reshape(bt, d)
    xs = x_sample.reshape(nb_s, d)
    pp_all = p_prompt.reshape(depth, bt, ple)
    ps_all = p_sample.reshape(depth, nb_s, ple)

    cache_k2 = cache_k.reshape(depth, -1, ps * nh, HEAD_DIM)
    cache_v2 = cache_v.reshape(depth, -1, ps * nh, HEAD_DIM)
    past_bias = decay_prefix(cache_lf, page_table).reshape(depth, nb_s, n_pages, 1, ps * nh)
    h0_re = state_ssm_re.reshape(depth, nb_s, -1)
    h0_im = state_ssm_im.reshape(depth, nb_s, -1)
    w_router = jnp.concatenate([router_group_w, router_expert_w,
                                jnp.zeros((depth, d, LANES - n_groups - n_experts), F32)], axis=2)
    b_router = jnp.concatenate([router_group_b, router_expert_b,
                                jnp.zeros((depth, LANES - n_groups - n_experts), F32)], axis=1)
    rest_col0 = 3 * da + nh
    n_rest = w_in.shape[2] - rest_col0
    assert n_rest == d_ssm + 2 * d and da % tn == 0 and d_ssm % tn == 0 and d % tn == 0

    ident = lambda accs, ex: accs
    glu = lambda accs, ex: [ex[0] * jax.nn.sigmoid(accs[0])]
    gated = lambda accs, ex: [jax.nn.sigmoid(ex[0]) * accs[0] + jax.nn.sigmoid(ex[1]) * accs[1]]
    resid = lambda accs, ex: [ex[0] + accs[0]]
    ple_gate = lambda accs, ex: [ex[0] + jax.nn.sigmoid(accs[0]) * accs[1]]

    def project(x, i, cfg):
        m, tmc, tr, _, hi = cfg
        act = F32 if hi else BF16
        h = rmsnorm_rows(x, g_mix[i], tr, act)
        (qkv,) = fused_matmul([(h, 0, w_in, i, 0)], [], ident, [F32], m, 3 * da, tmc, tn, hi)
        (f_logit,) = fused_matmul([(h, 0, w_in, i, 3 * da // LANES)], [], ident, [F32], m, LANES, tmc, LANES, hi)
        (rest,) = fused_matmul([(h, 0, w_in, i, 3 * da // tn)], [], ident, [F32], m, n_rest, tmc, tn, hi,
                               shift=nh)
        b_f_pad = jnp.concatenate([b_f[i], jnp.zeros((LANES - nh,), F32)]).reshape(1, LANES)
        return (qkv, rest) + tuple(qk_prepare(qkv, f_logit, g_q[i], g_k[i], b_f_pad, tr, nh, act))

    def mix_and_ffn(x, p_all, att, g_f, g_act, rest, i, cfg):
        m, tmc, tr, moe_rows, hi = cfg
        act = F32 if hi else BF16
        (s_out,) = fused_matmul([(g_act, 0, w_glu, i, 0)], [(g_f, 0)], glu, [act], m, d_ssm, tmc, tn, hi)
        (merged,) = fused_matmul([(s_out, 0, w_br_ssm, i, 0), (att, 0, w_br_att, i, 0)],
                                 [(rest, d_ssm // tn_m), (rest, (d_ssm + d) // tn_m)], gated, [act],
                                 m, d, tmc, tn_m, hi)
        (x,) = fused_matmul([(merged, 0, w_out, i, 0)], [(x, 0)], resid, [F32], m, d, tmc, tn, hi)
        hn, eid, wts = moe_route(x, g_ffn[i], w_router[i], b_router[i], n_groups, per_group, tr)
        src_token, slot_rows, tiles = moe_schedule(eid, n_experts, moe_rows)
        x_sorted = gather_rows(hn, src_token, tiles[2], moe_rows, d // LANES)
        hid = moe_up(x_sorted, w_gate, w_up, i, tiles, moe_rows, hi)
        y_sorted = moe_down(hid, w_down, i, tiles, moe_rows, hi)
        slot_step = math.gcd(DMA_ROWS, slot_rows.shape[0])
        y_slots = gather_rows(y_sorted, slot_rows, jnp.ones((slot_rows.shape[0] // slot_step,), I32), slot_step,
                              d // LANES)
        x, hn2 = moe_combine(x, y_slots, wts, g_ple[i], tr, act)
        (x,) = fused_matmul([(hn2, 0, w_ple_gate, i, 0), (p_all, i, w_ple_proj, i, 0)], [(x, 0)], ple_gate,
                            [F32], m, d, tmc, tn_m, hi)
        return x

    outs = [[] for _ in range(10)]
    for i in range(depth):
        qkv_p, rest_p, qn, kn_p, kb, vb, lf_p = project(xp, i, cfg_p)
        qkv_s, rest_s, qn_s, kn_s, _, vb_s, lf_s = project(xs, i, cfg_s)

        c = cumsum_time(lf_p, nb_p, t, min(256, t))
        tq = min(ATT_Q, t)
        ck_row = c[:, :nh].reshape(nb_p, t, nh).transpose(0, 2, 1).reshape(nb_p, nh, t // tq, 1, tq)
        att_p = prompt_attention(qn, kb, vb, c, ck_row, nb_p, t, nh, bt)
        tok3 = lambda a: a.reshape(nb_s, nh, HEAD_DIM)
        lf_new = jnp.tile(lf_s[:, :nh], (1, ps)).reshape(nb_s, 1, ps * nh)
        att_s = decode_attention(tok3(qn_s), tok3(kn_s), tok3(vb_s), lf_new, past_bias[i], cache_k2, cache_v2,
                                 page_table, i).reshape(nb_s, da)

        bcat, ccat, coef, ab, dskip = s5_parameters(ssm_a_re[i], ssm_a_im[i], ssm_log_dt[i], ssm_b_re[i],
                                                    ssm_b_im[i], ssm_c_re[i], ssm_c_im[i], ssm_d[i])
        gp_f, gp_b, hp_re, hp_im = s5_scan(rest_p, bcat.astype(BF16), ccat.astype(BF16), coef, ab, dskip,
                                           nb_p, t, bt, d_ssm)
        gs_f, _, hs_re, hs_im = s5_step(rest_s, 0, nb_s, nb_s, h0_re[i], h0_im[i], bcat, ccat, coef, ab, dskip,
                                        d_ssm)

        xp = mix_and_ffn(xp, pp_all, att_p, gp_f, gp_b, rest_p, i, cfg_p)
        xs = mix_and_ffn(xs, ps_all, att_s, gs_f, gs_f, rest_s, i, cfg_s)

        per_layer = (kn_p.reshape(nb_p, t, nh, HEAD_DIM), qkv_p[:, 2 * da:].reshape(nb_p, t, nh, HEAD_DIM),
                     lf_p[:, :nh].reshape(nb_p, t, nh),
                     hp_re.reshape(nb_p, n_grp, n_state), hp_im.reshape(nb_p, n_grp, n_state),
                     kn_s.reshape(nb_s, 1, nh, HEAD_DIM), qkv_s[:, 2 * da:].reshape(nb_s, 1, nh, HEAD_DIM),
                     lf_s[:, :nh].reshape(nb_s, 1, nh),
                     hs_re.reshape(nb_s, n_grp, n_state), hs_im.reshape(nb_s, n_grp, n_state))
        for lst, val in zip(outs, per_layer):
            lst.append(val)

    return (xp.reshape(nb_p, t, d), xs.reshape(nb_s, 1, d)) + tuple(jnp.stack(lst) for lst in outs)
```
